```python
import math
import jax
import jax.numpy as jnp
from jax import lax
import numpy as np

D_MODEL = 1024
BATCH = 8
SEQ = 2048
DEPTH = 2

GRID_W = 64
CTX_LEN = 256
MIX_WIDTH = D_MODEL
N_BRANCH = 3
HEAD_DIM = 128
ATT_HEADS = MIX_WIDTH // HEAD_DIM
ATT_KV_HEADS = ATT_HEADS // 4
ATT_WIDTH = ATT_HEADS * HEAD_DIM
KV_WIDTH = ATT_KV_HEADS * HEAD_DIM
Q_BLOCK = 128
ROPE_THETA = 10000.0
HG_KDIM = 128
HG_HEADS = MIX_WIDTH // HG_KDIM
HG_VDIM = MIX_WIDTH // HG_HEADS
HG_KWIDTH = HG_HEADS * HG_KDIM
HG_VWIDTH = HG_HEADS * HG_VDIM
HG_CHUNK = 64
HG_MIN_F = 1e-6
HY_WIDTH = MIX_WIDTH
HY_ORDER = 2
HY_SHORT = 3
HY_BANDS = 16
HY_EMB = 1 + 2 * HY_BANDS
HY_FILTER_HIDDEN = 64
HY_FAST_DECAY = 0.3
HY_SLOW_DECAY = 1.5
HY_TARGET = 1e-2
HY_MIN_DECAY = math.log(HY_TARGET) / HY_SLOW_DECAY
HY_MAX_DECAY = math.log(HY_TARGET) / HY_FAST_DECAY
D_FF = 4 * D_MODEL
EPS = 1e-6
F32 = jnp.float32
IN_SIZES = (ATT_WIDTH, KV_WIDTH, KV_WIDTH,
            HG_KWIDTH, HG_KWIDTH, HG_KWIDTH, HG_VWIDTH, HG_VWIDTH,
            (HY_ORDER + 1) * HY_WIDTH,
            N_BRANCH * D_MODEL)
IN_TOTAL = (ATT_WIDTH + 2 * KV_WIDTH + 3 * HG_KWIDTH + 2 * HG_VWIDTH
            + (HY_ORDER + 1) * HY_WIDTH + N_BRANCH * D_MODEL)

kernel_name = "hybrid_flow_backbone"


def rms_norm(x, g):
    xf = x.astype(F32)
    y = xf * lax.rsqrt(jnp.mean(xf * xf, axis=-1, keepdims=True) + EPS)
    return (y * g.astype(F32)).astype(x.dtype)


def adaln_modulation(cvec, w, b):
    m = jnp.einsum('...d,de->...e', jax.nn.silu(cvec), w) + b
    return jnp.split(m, 6, axis=-1)


def split_in(z):
    out, start = [], 0
    for size in IN_SIZES:
        out.append(z[..., start:start + size])
        start += size
    return out


def heads(t, n):
    b, l, _ = t.shape
    return t.reshape(b, l, n, -1).transpose(0, 2, 1, 3)


def axial_rope_tables(row, col):
    half = HEAD_DIM // 2
    inv = ROPE_THETA ** (-jnp.arange(0, half, 2, dtype=F32) / half)
    ar = row.astype(F32)[:, None] * inv
    ac = col.astype(F32)[:, None] * inv
    ang = jnp.concatenate([ar, ar, ac, ac], axis=-1)
    return jnp.cos(ang), jnp.sin(ang)


def apply_axial_rope(x, cos, sin):
    def rot_half(u):
        u1, u2 = jnp.split(u, 2, axis=-1)
        return jnp.concatenate([-u2, u1], axis=-1)
    xr, xc = jnp.split(x, 2, axis=-1)
    rx = jnp.concatenate([rot_half(xr), rot_half(xc)], axis=-1)
    return (x * cos + rx * sin).astype(x.dtype)


def attend(qb, keys, vals):
    s = jnp.einsum('bkgqd,bksd->bkgqs', qb, keys).astype(F32) * (HEAD_DIM ** -0.5)
    p = jax.nn.softmax(s, axis=-1).astype(vals.dtype)
    return jnp.einsum('bkgqs,bksd->bkgqd', p, vals)


def attention_mixer(q_l, k_l, v_l, q_c, k_c, v_c, gq, gk, cos, sin, need_ctx):
    b, n_lat, _ = q_l.shape
    grp = ATT_HEADS // ATT_KV_HEADS
    ql = apply_axial_rope(rms_norm(heads(q_l, ATT_HEADS), gq), cos, sin)
    kl = apply_axial_rope(rms_norm(heads(k_l, ATT_KV_HEADS), gk), cos, sin)
    kc = rms_norm(heads(k_c, ATT_KV_HEADS), gk)
    vl = heads(v_l, ATT_KV_HEADS)
    vc = heads(v_c, ATT_KV_HEADS)
    keys = jnp.concatenate([kl, kc], axis=2)
    vals = jnp.concatenate([vl, vc], axis=2)
    nblk = n_lat // Q_BLOCK
    qb = ql.reshape(b, ATT_KV_HEADS, grp, nblk, Q_BLOCK, HEAD_DIM).transpose(3, 0, 1, 2, 4, 5)
    ol = lax.map(lambda blk: attend(blk, keys, vals), qb)
    out_l = ol.transpose(1, 0, 4, 2, 3, 5).reshape(b, n_lat, ATT_WIDTH)
    out_c = None
    if need_ctx:
        n_ctx = q_c.shape[1]
        qc = rms_norm(heads(q_c, ATT_HEADS), gq).reshape(b, ATT_KV_HEADS, grp, n_ctx, HEAD_DIM)
        out_c = attend(qc, kc, vc).transpose(0, 3, 1, 2, 4).reshape(b, n_ctx, ATT_WIDTH)
    return out_l, out_c


def hgrn2_chunk_scan(q, k, v, log_f, s0):
    b, h, n, kd = q.shape
    vd = v.shape[-1]
    nc = n // HG_CHUNK
    resh = lambda a: a.reshape(b, h, nc, HG_CHUNK, a.shape[-1]).transpose(2, 0, 1, 3, 4)
    tri = jnp.tril(jnp.ones((HG_CHUNK, HG_CHUNK), dtype=bool))[:, :, None]

    def step(state, inp):
        qb, kb, vb, fb = inp
        cum = jnp.cumsum(fb, axis=2)
        o_inter = jnp.einsum('bhck,bhkv->bhcv', qb * jnp.exp(cum), state)
        diff = cum[:, :, :, None, :] - cum[:, :, None, :, :]
        decay = jnp.where(tri, jnp.exp(jnp.where(tri, diff, 0.0)), 0.0)
        scores = jnp.einsum('bhtk,bhsk,bhtsk->bhts', qb, kb, decay)
        o = o_inter + jnp.einsum('bhts,bhsv->bhtv', scores, vb)
        last = cum[:, :, -1:, :]
        new_state = state * jnp.exp(last[:, :, 0, :, None]) + jnp.einsum(
            'bhsk,bhsv->bhkv', kb * jnp.exp(last - cum), vb)
        return new_state, o

    s_fin, oc = lax.scan(step, s0, (resh(q), resh(k), resh(v), resh(log_f)))
    return oc.transpose(1, 2, 0, 3, 4).reshape(b, h, n, vd), s_fin


def hgrn2_direction(q, i, z, lb, s0):
    f = lb + (1.0 - lb) * jax.nn.sigmoid(z)
    log_f = jnp.log(jnp.maximum(f, HG_MIN_F))
    k = (1.0 - lb) * jax.nn.sigmoid(-z)
    return hgrn2_chunk_scan(q, k, i, log_f, s0)


def hgrn2_readout(o, g, g_norm):
    b, h, n, vd = o.shape
    o = rms_norm(o, g_norm).transpose(0, 2, 1, 3).reshape(b, n, h * vd)
    return (o * jax.nn.silu(g.astype(F32))).astype(g.dtype)


def hgrn2_mixer(q_l, zf_l, zb_l, i_l, g_l, q_c, zf_c, zb_c, i_c, g_c, lb, g_norm, need_ctx):
    to_heads = lambda t: heads(t, HG_HEADS).astype(F32)
    lb_f = lb[0].reshape(HG_HEADS, 1, HG_KDIM)
    lb_b = lb[1].reshape(HG_HEADS, 1, HG_KDIM)
    rev = lambda t: jnp.flip(t, axis=2)
    s0 = jnp.zeros((q_l.shape[0], HG_HEADS, HG_KDIM, HG_VDIM), F32)
    qC, fC, bC, iC = [to_heads(t) for t in (q_c, zf_c, zb_c, i_c)]
    qL, fL, bL, iL = [to_heads(t) for t in (q_l, zf_l, zb_l, i_l)]
    o_cf, s_cf = hgrn2_direction(qC, iC, fC, lb_f, s0)
    o_cb, s_cb = hgrn2_direction(rev(qC), rev(iC), rev(bC), lb_b, s0)
    o_lf, _ = hgrn2_direction(qL, iL, fL, lb_f, s_cf)
    o_lb, _ = hgrn2_direction(rev(qL), rev(iL), rev(bL), lb_b, s_cb)
    out_l = hgrn2_readout(o_lf + rev(o_lb), g_l, g_norm)
    out_c = hgrn2_readout(o_cf + rev(o_cb), g_c, g_norm) if need_ctx else None
    return out_l, out_c


def hyena_filters(n, w1, b1, w2, b2, w3, freq):
    t = jnp.linspace(0.0, 1.0, n, dtype=F32)[:, None]
    w = (2.0 * math.pi / n) * jnp.arange(n, dtype=F32)[:, None]
    f = jnp.linspace(1e-4, HY_BANDS - 1, HY_BANDS, dtype=F32)[None, :]
    z = jnp.concatenate([t, jnp.cos(f * w), -jnp.sin(f * w)], axis=-1)
    h = jnp.sin(freq[0] * (z @ w1 + b1))
    h = jnp.sin(freq[1] * (h @ w2 + b2))
    h = (h @ w3).astype(F32).reshape(n, HY_ORDER, 2, HY_WIDTH)
    deltas = jnp.abs(jnp.linspace(HY_MIN_DECAY, HY_MAX_DECAY, HY_WIDTH, dtype=F32))
    h = h * jnp.exp(-t * deltas)[:, None, None, :]
    fwd, bwd = h[:, :, 0], h[:, :, 1]
    filt2 = jnp.concatenate([fwd, jnp.zeros_like(fwd[:1]), bwd[:0:-1]], axis=0)
    return filt2 / (jnp.sum(jnp.abs(filt2), axis=0, keepdims=True) + EPS)


def short_conv(u, w, b):
    n = u.shape[1]
    up = jnp.pad(u, ((0, 0), (1, 1), (0, 0)))
    return up[:, :n] * w[0] + up[:, 1:n + 1] * w[1] + up[:, 2:] * w[2] + b


def hyena_mixer(u, sw, sb, filt2, bias):
    u = short_conv(u, sw, sb).astype(F32)
    x1, x2, v = jnp.split(u, 3, axis=-1)
    n = u.shape[1]
    filt_f = jnp.fft.rfft(filt2, axis=0)
    z = v
    for o, gate in enumerate((x1, x2)):
        zf = jnp.fft.rfft(z, n=2 * n, axis=1)
        zc = jnp.fft.irfft(zf * filt_f[None, :, o], n=2 * n, axis=1)[:, :n]
        z = gate * (zc + z * bias[o].astype(F32))
    return z


def merge_branches(att, hg, hy, gates, wb):
    ga, gh, gy = jnp.split(jax.nn.sigmoid(gates.astype(F32)).astype(gates.dtype), N_BRANCH, axis=-1)
    return ga * (att @ wb[0]) + gh * (hg @ wb[1]) + gy * (hy @ wb[2])


def sq_relu_mlp(h, w1, w2):
    return jnp.square(jax.nn.relu(h @ w1)) @ w2


def setup_inputs(seed: int = 0) -> dict:
    key = jax.random.key(seed)
    ks = jax.random.split(key, 32)
    nrm = lambda k, shape, s: jax.random.normal(k, shape, F32) * s
    return {
        'x': nrm(ks[0], (BATCH, SEQ, D_MODEL), 1.0),
        'c': nrm(ks[1], (BATCH, D_MODEL), 1.0),
        'ctx': nrm(ks[2], (BATCH, CTX_LEN, D_MODEL), 1.0),
        'c_ctx': nrm(ks[3], (D_MODEL,), 1.0),
        'ada_w': nrm(ks[4], (DEPTH, D_MODEL, 6 * D_MODEL), 0.5 * D_MODEL ** -0.5),
        'ada_b': nrm(ks[5], (DEPTH, 6 * D_MODEL), 0.02),
        'norm1_g': 1.0 + nrm(ks[6], (DEPTH, D_MODEL), 0.02),
        'norm2_g': 1.0 + nrm(ks[7], (DEPTH, D_MODEL), 0.02),
        'w_in': nrm(ks[8], (DEPTH, D_MODEL, IN_TOTAL), D_MODEL ** -0.5),
        'q_norm_g': 1.0 + nrm(ks[9], (DEPTH, HEAD_DIM), 0.02),
        'k_norm_g': 1.0 + nrm(ks[10], (DEPTH, HEAD_DIM), 0.02),
        'hg_lb_raw': nrm(ks[11], (DEPTH, 2, HG_KWIDTH), 0.1),
        'hg_norm_g': 1.0 + nrm(ks[12], (DEPTH, HG_VDIM), 0.02),
        'hy_short_w': nrm(ks[13], (DEPTH, HY_SHORT, (HY_ORDER + 1) * HY_WIDTH), HY_SHORT ** -0.5),
        'hy_short_b': nrm(ks[14], (DEPTH, (HY_ORDER + 1) * HY_WIDTH), 0.02),
        'hy_filt_w1': nrm(ks[15], (DEPTH, HY_EMB, HY_FILTER_HIDDEN), HY_EMB ** -0.5),
        'hy_filt_b1': nrm(ks[16], (DEPTH, HY_FILTER_HIDDEN), 0.02),
        'hy_filt_w2': nrm(ks[17], (DEPTH, HY_FILTER_HIDDEN, HY_FILTER_HIDDEN), HY_FILTER_HIDDEN ** -0.5),
        'hy_filt_b2': nrm(ks[18], (DEPTH, HY_FILTER_HIDDEN), 0.02),
        'hy_filt_w3': nrm(ks[19], (DEPTH, HY_FILTER_HIDDEN, HY_ORDER * 2 * HY_WIDTH), HY_FILTER_HIDDEN ** -0.5),
        'hy_freq': 1.0 + nrm(ks[20], (DEPTH, 2, HY_FILTER_HIDDEN), 0.02),
        'hy_bias': nrm(ks[21], (DEPTH, HY_ORDER, HY_WIDTH), 0.1),
        'w_branch': nrm(ks[22], (DEPTH, N_BRANCH, MIX_WIDTH, D_MODEL), MIX_WIDTH ** -0.5),
        'w_out': nrm(ks[23], (DEPTH, D_MODEL, D_MODEL), D_MODEL ** -0.5),
        'w_mlp1': nrm(ks[24], (DEPTH, D_MODEL, D_FF), D_MODEL ** -0.5),
        'w_mlp2': nrm(ks[25], (DEPTH, D_FF, D_MODEL), D_FF ** -0.5),
    }


def reference(x, c, ctx, c_ctx, ada_w, ada_b, norm1_g, norm2_g, w_in, q_norm_g, k_norm_g,
              hg_lb_raw, hg_norm_g, hy_short_w, hy_short_b, hy_filt_w1, hy_filt_b1,
              hy_filt_w2, hy_filt_b2, hy_filt_w3, hy_freq, hy_bias, w_branch, w_out,
              w_mlp1, w_mlp2):
    n_lat = x.shape[1]
    n_ctx = ctx.shape[1]
    rows = n_lat // GRID_W
    row = jnp.repeat(jnp.arange(rows, dtype=jnp.int32), GRID_W)
    col = jnp.tile(jnp.arange(GRID_W, dtype=jnp.int32), rows)
    cos, sin = axial_rope_tables(row, col)
    sm = jax.nn.softmax(hg_lb_raw.astype(F32), axis=0)
    lower = jnp.cumsum(sm, axis=0) - sm[0:1]

    xl, xc = x, ctx
    for l in range(DEPTH):
        need_ctx = l < DEPTH - 1
        sh1, sc1, ga1, sh2, sc2, ga2 = [m[:, None, :] for m in adaln_modulation(c, ada_w[l], ada_b[l])]
        csh1, csc1, cga1, csh2, csc2, cga2 = adaln_modulation(c_ctx, ada_w[l], ada_b[l])

        hl = rms_norm(xl, norm1_g[l]) * (1.0 + sc1) + sh1
        hc = rms_norm(xc, norm1_g[l]) * (1.0 + csc1) + csh1
        ql, kl, vl, hql, hfl, hbl, hil, hgl, hyl, gtl = split_in(hl @ w_in[l])
        qc, kc, vc, hqc, hfc, hbc, hic, hgc, hyc, gtc = split_in(hc @ w_in[l])

        att_l, att_c = attention_mixer(ql, kl, vl, qc, kc, vc, q_norm_g[l], k_norm_g[l], cos, sin, need_ctx)
        hg_l, hg_c = hgrn2_mixer(hql, hfl, hbl, hil, hgl, hqc, hfc, hbc, hic, hgc,
                                 lower[l], hg_norm_g[l], need_ctx)
        filt_l = hyena_filters(n_lat, hy_filt_w1[l], hy_filt_b1[l], hy_filt_w2[l], hy_filt_b2[l],
                               hy_filt_w3[l], hy_freq[l])
        hy_l = hyena_mixer(hyl, hy_short_w[l], hy_short_b[l], filt_l, hy_bias[l]).astype(xl.dtype)
        xl = xl + ga1 * (merge_branches(att_l, hg_l, hy_l, gtl, w_branch[l]) @ w_out[l])
        if need_ctx:
            filt_c = hyena_filters(n_ctx, hy_filt_w1[l], hy_filt_b1[l], hy_filt_w2[l], hy_filt_b2[l],
                                   hy_filt_w3[l], hy_freq[l])
            hy_c = hyena_mixer(hyc, hy_short_w[l], hy_short_b[l], filt_c, hy_bias[l]).astype(xc.dtype)
            xc = xc + cga1 * (merge_branches(att_c, hg_c, hy_c, gtc, w_branch[l]) @ w_out[l])

        xl = xl + ga2 * sq_relu_mlp(rms_norm(xl, norm2_g[l]) * (1.0 + sc2) + sh2, w_mlp1[l], w_mlp2[l])
        if need_ctx:
            xc = xc + cga2 * sq_relu_mlp(rms_norm(xc, norm2_g[l]) * (1.0 + csc2) + csh2,
                                         w_mlp1[l], w_mlp2[l])
    return xl
```

```python
import functools
import math

import jax
import jax.numpy as jnp
import numpy as np
from jax import lax
from jax.experimental import pallas as pl
from jax.experimental.pallas import tpu as pltpu

F32 = jnp.float32
BF16 = jnp.bfloat16

LANES = 128
HEAD_DIM = 128
ATT_HEADS = 8
ATT_KV_HEADS = 2
ATT_GROUP = ATT_HEADS // ATT_KV_HEADS
ROPE_THETA = 10000.0
GRID_W = 64
HG_HEADS = 8
HG_CHUNK = 64
HG_MIN_F = 1e-6
HY_BANDS = 16
HY_EMB = 1 + 2 * HY_BANDS
HY_FAST_DECAY = 0.3
HY_SLOW_DECAY = 1.5
HY_TARGET = 1e-2
HY_MIN_DECAY = math.log(HY_TARGET) / HY_SLOW_DECAY
HY_MAX_DECAY = math.log(HY_TARGET) / HY_FAST_DECAY
EPS = 1e-6
DFT_P = 256
HIGHEST = lax.Precision.HIGHEST

COL_GATES = 0
COL_ATT_Q = 24
COL_ATT_K = 32
COL_ATT_V = 34
COL_HG_Q = 36
COL_HG_F = 44
COL_HG_B = 52
COL_HG_I = 60
COL_HG_G = 68
COL_HY = 76
N_COLS = 100


def _pick(n, candidates):
    for c in candidates:
        if n % c == 0:
            return c
    raise ValueError(f"no tile in {candidates} divides {n}")


def _nt_dot(a, b):
    return lax.dot_general(a, b, (((1,), (1,)), ((), ())), preferred_element_type=F32)


def _adaln_kernel(c_ref, w_ref, b_ref, o_ref):
    cv = c_ref[...]
    s = cv * jax.nn.sigmoid(cv)
    o_ref[0] = jnp.dot(s, w_ref[0], precision=HIGHEST, preferred_element_type=F32) + b_ref[0]


def _adaln(cvec, ada_w, ada_b):
    depth, d, n6 = ada_w.shape
    rows = cvec.shape[0]
    tn = _pick(n6, (1536, 1024, 512, 128))
    return pl.pallas_call(
        _adaln_kernel,
        grid=(depth, n6 // tn),
        in_specs=[pl.BlockSpec((rows, d), lambda l, j: (0, 0)),
                  pl.BlockSpec((1, d, tn), lambda l, j: (l, 0, j)),
                  pl.BlockSpec((1, 1, tn), lambda l, j: (l, 0, j))],
        out_specs=pl.BlockSpec((1, rows, tn), lambda l, j: (l, 0, j)),
        out_shape=jax.ShapeDtypeStruct((depth, rows, n6), F32),
        name="adaln",
    )(cvec, ada_w, ada_b.reshape(depth, 1, n6))


def _modulated_norm(x, g, sh_l, sc_l, sh_c, sc_c, row0, n_lat):
    y = x * lax.rsqrt(jnp.mean(x * x, axis=-1, keepdims=True) + EPS) * g
    row = row0 + lax.broadcasted_iota(jnp.int32, (x.shape[0], 1), 0)
    is_ctx = row >= n_lat
    sc = jnp.where(is_ctx, sc_c, sc_l)
    sh = jnp.where(is_ctx, sh_c, sh_l)
    return y * (1.0 + sc) + sh


def _mod_specs(layer, ctx_row, k_shift, k_scale, d, nargs):
    def spec(row_fn, k):
        if nargs == 3:
            return pl.BlockSpec((1, 1, 1, d), lambda b, t, j: (layer, row_fn(b), 0, k))
        return pl.BlockSpec((1, 1, 1, d), lambda b, t: (layer, row_fn(b), 0, k))
    lat = lambda b: b
    ctx = lambda b: ctx_row
    return [spec(lat, k_shift), spec(lat, k_scale), spec(ctx, k_shift), spec(ctx, k_scale)]


def _inproj_kernel(x_ref, shl_ref, scl_ref, shc_ref, scc_ref, g_ref, w_ref, o_ref, h_ref, *, n_lat, tm):
    t = pl.program_id(1)

    @pl.when(pl.program_id(2) == 0)
    def _():
        h = _modulated_norm(x_ref[0], g_ref[...], shl_ref[0, 0], scl_ref[0, 0], shc_ref[0, 0], scc_ref[0, 0],
                            t * tm, n_lat)
        h_ref[...] = h.astype(BF16)

    o_ref[0] = jnp.dot(h_ref[...], w_ref[...], preferred_element_type=F32).astype(BF16)


def _inproj(x, mod, layer, ctx_row, g, w_bf, n_lat):
    bsz, seq, d = x.shape
    n = w_bf.shape[1]
    tm = _pick(seq, (768, 512, 256))
    tn = _pick(n, (2560, 1280, 640, 128))
    return pl.pallas_call(
        functools.partial(_inproj_kernel, n_lat=n_lat, tm=tm),
        grid=(bsz, seq // tm, n // tn),
        in_specs=[pl.BlockSpec((1, tm, d), lambda b, t, j: (b, t, 0))]
        + _mod_specs(layer, ctx_row, 0, 1, d, 3)
        + [pl.BlockSpec((1, d), lambda b, t, j: (0, 0)),
           pl.BlockSpec((d, tn), lambda b, t, j: (0, j))],
        out_specs=pl.BlockSpec((1, tm, tn), lambda b, t, j: (b, t, j)),
        out_shape=jax.ShapeDtypeStruct((bsz, seq, n), BF16),
        scratch_shapes=[pltpu.VMEM((tm, d), BF16)],
        compiler_params=pltpu.CompilerParams(dimension_semantics=("parallel", "parallel", "arbitrary")),
        name="inproj",
    )(x, mod, mod, mod, mod, g.reshape(1, d), w_bf)


def _rope(x, cos, sin_signed):
    lane = lax.broadcasted_iota(jnp.int32, x.shape, 1)
    first = (lane % 64) < 32
    rx = jnp.where(first, pltpu.roll(x, 96, 1), pltpu.roll(x, 32, 1))
    return x * cos + rx * sin_signed


def _head_norm(x, g):
    return x * lax.rsqrt(jnp.mean(x * x, axis=-1, keepdims=True) + EPS) * g


def _attn_kernel(q_ref, k_ref, v_ref, cq_ref, sq_ref, ck_ref, sk_ref, gq_ref, gk_ref, o_ref, ks_ref,
                 *, n_lat, tq):
    qi = pl.program_id(2)

    @pl.when(qi == 0)
    def _():
        k = _head_norm(k_ref[0].astype(F32), gk_ref[...])
        ks_ref[...] = _rope(k, ck_ref[...], sk_ref[...]).astype(BF16)

    scale = HEAD_DIM ** -0.5
    qall = q_ref[0].astype(F32)
    qs = []
    for g in range(ATT_GROUP):
        qg = _head_norm(qall[:, g * HEAD_DIM:(g + 1) * HEAD_DIM], gq_ref[...])
        qs.append((_rope(qg, cq_ref[...], sq_ref[...]) * scale).astype(BF16))
    qstack = jnp.concatenate(qs, axis=0)

    def attend(keys, vals):
        s = _nt_dot(qstack, keys)
        p = jnp.exp(s - jnp.max(s, axis=-1, keepdims=True))
        o = jnp.dot(p.astype(BF16), vals, preferred_element_type=F32)
        o = o / jnp.sum(p, axis=-1, keepdims=True)
        for g in range(ATT_GROUP):
            o_ref[0, :, g * HEAD_DIM:(g + 1) * HEAD_DIM] = o[g * tq:(g + 1) * tq].astype(BF16)

    is_lat = qi * tq < n_lat

    @pl.when(is_lat)
    def _():
        attend(ks_ref[...], v_ref[0])

    @pl.when(jnp.logical_not(is_lat))
    def _():
        attend(ks_ref[n_lat:, :], v_ref[0, n_lat:, :])


def _attention(z, cos, sin_signed, gq, gk, n_lat, need_ctx):
    bsz, seq, _ = z.shape
    tq = 256
    n_q = (seq if need_ctx else n_lat) // tq
    gw = ATT_GROUP * HEAD_DIM
    return pl.pallas_call(
        functools.partial(_attn_kernel, n_lat=n_lat, tq=tq),
        grid=(bsz, ATT_KV_HEADS, n_q),
        in_specs=[pl.BlockSpec((1, tq, gw), lambda b, h, i: (b, i, COL_ATT_Q // ATT_GROUP + h)),
                  pl.BlockSpec((1, seq, HEAD_DIM), lambda b, h, i: (b, 0, COL_ATT_K + h)),
                  pl.BlockSpec((1, seq, HEAD_DIM), lambda b, h, i: (b, 0, COL_ATT_V + h)),
                  pl.BlockSpec((tq, HEAD_DIM), lambda b, h, i: (i, 0)),
                  pl.BlockSpec((tq, HEAD_DIM), lambda b, h, i: (i, 0)),
                  pl.BlockSpec((seq, HEAD_DIM), lambda b, h, i: (0, 0)),
                  pl.BlockSpec((seq, HEAD_DIM), lambda b, h, i: (0, 0)),
                  pl.BlockSpec((1, HEAD_DIM), lambda b, h, i: (0, 0)),
                  pl.BlockSpec((1, HEAD_DIM), lambda b, h, i: (0, 0))],
        out_specs=pl.BlockSpec((1, tq, gw), lambda b, h, i: (b, i, h)),
        out_shape=jax.ShapeDtypeStruct((bsz, seq, ATT_HEADS * HEAD_DIM), BF16),
        scratch_shapes=[pltpu.VMEM((seq, HEAD_DIM), BF16)],
        compiler_params=pltpu.CompilerParams(dimension_semantics=("parallel", "parallel", "arbitrary")),
        name="attention",
    )(z, z, z, cos, sin_signed, cos, sin_signed, gq.reshape(1, HEAD_DIM), gk.reshape(1, HEAD_DIM))


def _rope_tables(n_lat, n_ctx):
    half = HEAD_DIM // 2
    inv = ROPE_THETA ** (-np.arange(0, half, 2, dtype=np.float64) / half)
    t = np.arange(n_lat)
    ar = (t // GRID_W)[:, None] * inv
    ac = (t % GRID_W)[:, None] * inv
    ang = np.concatenate([ar, ar, ac, ac], axis=-1)
    cos = np.concatenate([np.cos(ang), np.ones((n_ctx, HEAD_DIM))], axis=0)
    sin = np.concatenate([np.sin(ang), np.zeros((n_ctx, HEAD_DIM))], axis=0)
    sign = np.where((np.arange(HEAD_DIM) % 64) < 32, -1.0, 1.0)
    return jnp.asarray(cos, F32), jnp.asarray(sin * sign, F32)


_HG_ROWS = 6 * HG_CHUNK + 16


def _hgrn_constants():
    c = HG_CHUNK
    idx = np.arange(c)
    tri = (idx[None, :] <= idx[:, None]).astype(np.float32)

    def halving(size):
        half = size // 2
        w = np.zeros((c, c), np.float32)
        for r in range(c):
            a = (r // size) * size
            if r >= a + half:
                w[r, a + half:r + 1] = 1.0
            else:
                w[r, r + 1:a + half] = 1.0
        return w

    diag = np.zeros((c, c), np.float32)
    for r in range(c):
        mid = (r // 8) * 8 + 3
        if r <= mid:
            diag[r, r + 1:mid + 1] = -1.0
        else:
            diag[r, mid + 1:r + 1] = 1.0
    suffix = (idx[None, :] > idx[:, None]).astype(np.float32)
    blocks = [tri, halving(64), halving(32), halving(16), diag, suffix]
    total = np.ones((16, c), np.float32)
    fwd = np.concatenate(blocks + [total], axis=0)
    bwd = np.concatenate([blk[::-1, ::-1] for blk in blocks] + [total], axis=0)

    level = np.zeros((c, c), np.int32)
    t, s = idx[:, None], idx[None, :]
    for code, size in ((1, 64), (2, 32), (3, 16)):
        half = size // 2
        sel = (t // size == s // size) & (t % size >= half) & (s % size < half)
        level[sel] = code
    level[(t // 8 == s // 8) & (s <= t)] = 4
    mats = np.stack([fwd, bwd])
    levels = np.stack([level, level[::-1, ::-1]])
    return jnp.asarray(mats, BF16), jnp.asarray(levels, jnp.int32)


def _hgrn_chunk(q, zg, v, st, lb, mat, level):
    c = HG_CHUNK
    qf = q.astype(F32)
    sig = jax.nn.sigmoid(zg.astype(F32))
    lf = jnp.log(jnp.maximum(lb + (1.0 - lb) * sig, HG_MIN_F))
    kk = (1.0 - lb) * (1.0 - sig)
    hi = lf.astype(BF16)
    r1 = lf - hi.astype(F32)
    mid = r1.astype(BF16)
    lo = (r1 - mid.astype(F32)).astype(BF16)
    e3 = jnp.dot(mat, jnp.concatenate([hi, mid, lo], axis=1), preferred_element_type=F32)
    ex = e3[:, :LANES] + e3[:, LANES:2 * LANES] + e3[:, 2 * LANES:]
    e_cum = jnp.exp(ex[0:c])
    e_suf = jnp.exp(ex[5 * c:6 * c])
    e_tot = jnp.exp(ex[6 * c:6 * c + 1])
    out = _nt_dot((qf * e_cum).astype(BF16), st.astype(BF16))
    scores = jnp.zeros((c, c), F32)
    for code in (4, 3, 2, 1):
        lo_r = code * c
        if code == 4:
            eq = jnp.exp(ex[lo_r:lo_r + c])
            ek = jnp.exp(-ex[lo_r:lo_r + c])
        else:
            eq = ek = jnp.exp(ex[lo_r:lo_r + c])
        sc = _nt_dot((qf * eq).astype(BF16), (kk * ek).astype(BF16))
        scores = jnp.where(level == code, sc, scores)
    out = out + jnp.dot(scores.astype(BF16), v, preferred_element_type=F32)
    upd = lax.dot_general(v, (kk * e_suf).astype(BF16), (((0,), (0,)), ((), ())), preferred_element_type=F32)
    return out, st * e_tot + upd


def _hgrn_kernel(q_ref, zf_ref, zb_ref, i_ref, g_ref, lbraw_ref, gn_ref, mat_ref, lev_ref, o_ref,
                 acc_ref, st_ref, *, layer, n_lat, n_ctx):
    c = HG_CHUNK
    nc_lat, nc_ctx = n_lat // c, n_ctx // c
    nc = nc_lat + nc_ctx
    raw = lbraw_ref[:, 0].astype(F32)
    ew = jnp.exp(raw - jnp.max(raw, axis=0, keepdims=True))
    sm = ew / jnp.sum(ew, axis=0, keepdims=True)
    lower = jnp.zeros_like(sm[0])
    for dpt in range(1, layer + 1):
        lower = lower + sm[dpt]
    lb_f, lb_b = lower[0:1], lower[1:2]

    acc_ref[...] = jnp.zeros_like(acc_ref)
    st_ref[...] = jnp.zeros_like(st_ref)

    def step(n, carry):
        cf = jnp.where(n < nc_ctx, nc_lat + n, n - nc_ctx)
        cb = nc - 1 - n
        for d, (chunk, z_ref, lb) in enumerate(((cf, zf_ref, lb_f), (cb, zb_ref, lb_b))):
            rows = pl.ds(pl.multiple_of(chunk * c, c), c)
            out, st = _hgrn_chunk(q_ref[0, rows, :], z_ref[0, rows, :], i_ref[0, rows, :], st_ref[d],
                                  lb, mat_ref[d], lev_ref[d])
            st_ref[d] = st
            acc_ref[rows, :] = acc_ref[rows, :] + out
        return carry

    lax.fori_loop(0, nc, step, 0)

    o = acc_ref[...]
    y = o * lax.rsqrt(jnp.mean(o * o, axis=-1, keepdims=True) + EPS) * gn_ref[...]
    g = g_ref[0].astype(F32)
    o_ref[0] = (y * (g * jax.nn.sigmoid(g))).astype(BF16)


def _hgrn(z, lb_raw, g_norm, layer, n_lat, n_ctx):
    bsz, seq, _ = z.shape
    depth = lb_raw.shape[0]
    mats, levels = _hgrn_constants()
    lbr = lb_raw.reshape(depth, 2, HG_HEADS, LANES).transpose(0, 2, 1, 3)
    col = lambda c0: pl.BlockSpec((1, seq, LANES), lambda b, h: (b, 0, c0 + h))
    return pl.pallas_call(
        functools.partial(_hgrn_kernel, layer=layer, n_lat=n_lat, n_ctx=n_ctx),
        grid=(bsz, HG_HEADS),
        in_specs=[col(COL_HG_Q), col(COL_HG_F), col(COL_HG_B), col(COL_HG_I), col(COL_HG_G),
                  pl.BlockSpec((depth, 1, 2, LANES), lambda b, h: (0, h, 0, 0)),
                  pl.BlockSpec((1, LANES), lambda b, h: (0, 0)),
                  pl.BlockSpec((2, _HG_ROWS, HG_CHUNK), lambda b, h: (0, 0, 0)),
                  pl.BlockSpec((2, HG_CHUNK, HG_CHUNK), lambda b, h: (0, 0, 0))],
        out_specs=pl.BlockSpec((1, seq, LANES), lambda b, h: (b, 0, h)),
        out_shape=jax.ShapeDtypeStruct((bsz, seq, HG_HEADS * LANES), BF16),
        scratch_shapes=[pltpu.VMEM((seq, LANES), F32), pltpu.VMEM((2, LANES, LANES), F32)],
        compiler_params=pltpu.CompilerParams(dimension_semantics=("parallel", "parallel")),
        name="hgrn",
    )(z, z, z, z, z, lbr, g_norm.reshape(1, LANES), mats, levels)


def _dft_matrices(n):
    size = 2 * n
    k = np.arange(n)[:, None]
    t = np.arange(n)[None, :]
    ang = (2.0 * np.pi / size) * ((k * t) % size)
    cos, sin = np.cos(ang), np.sin(ang)
    sin[0, :] = np.where(np.arange(n) % 2 == 0, 1.0, -1.0)
    fwd = np.concatenate([cos.reshape(n // DFT_P, DFT_P, n), sin.reshape(n // DFT_P, DFT_P, n)], axis=1)
    fwd = fwd.reshape(size, n)
    return jnp.asarray(fwd, BF16), jnp.asarray(fwd.T.copy(), BF16)


def _hyena_positions(n):
    t = np.linspace(0.0, 1.0, n)[:, None]
    w = (2.0 * math.pi / n) * np.arange(n)[:, None]
    f = np.linspace(1e-4, HY_BANDS - 1, HY_BANDS)[None, :]
    z = np.concatenate([t, np.cos(f * w), -np.sin(f * w)], axis=-1)
    return jnp.asarray(np.pad(z, ((0, 0), (0, LANES - HY_EMB))), F32)


def _filter_kernel(pos_ref, w1_ref, b1_ref, w2_ref, b2_ref, fr_ref, w3f_ref, w3b_ref, dl_ref, fs_ref, fd_ref):
    pos = pos_ref[...]
    h = jnp.sin(fr_ref[0:1] * (jnp.dot(pos, w1_ref[...], precision=HIGHEST, preferred_element_type=F32)
                               + b1_ref[...]))
    h = jnp.sin(fr_ref[1:2] * (jnp.dot(h, w2_ref[...], precision=HIGHEST, preferred_element_type=F32)
                               + b2_ref[...]))
    window = jnp.exp(-pos[:, 0:1] * dl_ref[...])
    hf = jnp.dot(h, w3f_ref[...], precision=HIGHEST, preferred_element_type=F32) * window
    hb = jnp.dot(h, w3b_ref[...], precision=HIGHEST, preferred_element_type=F32) * window
    row = lax.broadcasted_iota(jnp.int32, (pos.shape[0], 1), 0)
    hb = jnp.where(row == 0, 0.0, hb)
    norm = jnp.sum(jnp.abs(hf), axis=0, keepdims=True) + jnp.sum(jnp.abs(hb), axis=0, keepdims=True) + EPS
    inv = 1.0 / norm
    fs_ref[0] = ((hf + hb) * inv).astype(BF16)
    fd_ref[0] = ((hf - hb) * inv).astype(BF16)


def _filter_dft_kernel(a_ref, fs_ref, fd_ref, o_ref, *, n):
    m = pl.program_id(2)
    p = DFT_P
    hc = jnp.dot(a_ref[:p], fs_ref[0], preferred_element_type=F32)
    hs = jnp.dot(a_ref[p:], fd_ref[0], preferred_element_type=F32)
    k = m * p + lax.broadcasted_iota(jnp.int32, (p, 1), 0)
    weight = jnp.where(k == 0, 1.0, 2.0) / (2 * n)
    o_ref[0, :p] = hc * weight
    o_ref[0, p:] = hs * weight

    @pl.when(m == 0)
    def _():
        hny = jnp.dot(a_ref[p:p + 16], fs_ref[0], preferred_element_type=F32)[:8]
        first = lax.broadcasted_iota(jnp.int32, (8, 1), 0) == 0
        o_ref[0, p:p + 8] = jnp.where(first, hny / (2 * n), hs[:8] * weight[:8])


def _hyena_filters(n, w1p, b1, w2, b2, w3, freq):
    hid = w2.shape[0]
    c = w3.shape[1] // 4
    tn = _pick(c, (512, 256, 128))
    nct = c // tn
    pos = _hyena_positions(n)
    deltas = jnp.asarray(np.abs(np.linspace(HY_MIN_DECAY, HY_MAX_DECAY, c))[None, :], F32)
    full = lambda shape: pl.BlockSpec(shape, lambda o, j: (0,) * len(shape))
    fsum, fdiff = pl.pallas_call(
        _filter_kernel,
        grid=(2, nct),
        in_specs=[full((n, LANES)), full((LANES, hid)), full((1, hid)), full((hid, hid)), full((1, hid)),
                  full((2, hid)),
                  pl.BlockSpec((hid, tn), lambda o, j: (0, (2 * o) * nct + j)),
                  pl.BlockSpec((hid, tn), lambda o, j: (0, (2 * o + 1) * nct + j)),
                  pl.BlockSpec((1, tn), lambda o, j: (0, j))],
        out_specs=[pl.BlockSpec((1, n, tn), lambda o, j: (o, 0, j))] * 2,
        out_shape=[jax.ShapeDtypeStruct((2, n, c), BF16)] * 2,
        name="hyena_filter",
    )(pos, w1p, b1.reshape(1, hid), w2, b2.reshape(1, hid), freq, w3, w3, deltas)
    fwd, _ = _dft_matrices(n)
    return pl.pallas_call(
        functools.partial(_filter_dft_kernel, n=n),
        grid=(2, nct, n // DFT_P),
        in_specs=[pl.BlockSpec((2 * DFT_P, n), lambda o, j, m: (m, 0)),
                  pl.BlockSpec((1, n, tn), lambda o, j, m: (o, 0, j)),
                  pl.BlockSpec((1, n, tn), lambda o, j, m: (o, 0, j))],
        out_specs=pl.BlockSpec((1, 2 * DFT_P, tn), lambda o, j, m: (o, m, j)),
        out_shape=jax.ShapeDtypeStruct((2, 2 * n, c), F32),
        name="hyena_filter_dft",
    )(fwd, fsum, fdiff)


def _shortconv_kernel(u_ref, w_ref, b_ref, o_ref, *, n_lat):
    u = u_ref[0].astype(F32)
    seq = u.shape[0]
    row = lax.broadcasted_iota(jnp.int32, (seq, 1), 0)
    prev = jnp.where((row == 0) | (row == n_lat), 0.0, pltpu.roll(u, 1, 0))
    nxt = jnp.where((row == n_lat - 1) | (row == seq - 1), 0.0, pltpu.roll(u, seq - 1, 0))
    o_ref[0] = (prev * w_ref[0:1] + u * w_ref[1:2] + nxt * w_ref[2:3] + b_ref[...]).astype(BF16)


def _shortconv(z, w, b, n_lat):
    bsz, seq, _ = z.shape
    width = w.shape[1]
    tn = 256
    c0 = COL_HY * LANES // tn
    return pl.pallas_call(
        functools.partial(_shortconv_kernel, n_lat=n_lat),
        grid=(bsz, width // tn),
        in_specs=[pl.BlockSpec((1, seq, tn), lambda bb, j: (bb, 0, c0 + j)),
                  pl.BlockSpec((3, tn), lambda bb, j: (0, j)),
                  pl.BlockSpec((1, tn), lambda bb, j: (0, j))],
        out_specs=pl.BlockSpec((1, seq, tn), lambda bb, j: (bb, 0, j)),
        out_shape=jax.ShapeDtypeStruct((bsz, seq, width), BF16),
        compiler_params=pltpu.CompilerParams(dimension_semantics=("parallel", "parallel")),
        name="hyena_shortconv",
    )(z, w, b.reshape(1, width))


def _conv_fwd_kernel(a_ref, v_ref, h_ref, y_ref):
    p = DFT_P
    zf = jnp.dot(a_ref[...], v_ref[0], preferred_element_type=F32)
    zc, zs = zf[:p], zf[p:]
    hc, hs = h_ref[0, :p], h_ref[0, p:]
    real_row = (lax.broadcasted_iota(jnp.int32, (p, 1), 0) == 0) & (pl.program_id(2) == 0)
    y_ref[0, :p] = jnp.where(real_row, zc * hc, zc * hc - zs * hs).astype(BF16)
    y_ref[0, p:] = jnp.where(real_row, zs * hs, zc * hs + zs * hc).astype(BF16)


def _conv_fwd(fwd, v_arr, v_col, spec, order, n, row_blk):
    bsz = v_arr.shape[0]
    c = spec.shape[2]
    tn = 512
    return pl.pallas_call(
        _conv_fwd_kernel,
        grid=(c // tn, bsz, n // DFT_P),
        in_specs=[pl.BlockSpec((2 * DFT_P, n), lambda j, b, m: (m, 0)),
                  pl.BlockSpec((1, n, tn), lambda j, b, m: (b, row_blk, v_col + j)),
                  pl.BlockSpec((1, 2 * DFT_P, tn), lambda j, b, m: (order, m, j))],
        out_specs=pl.BlockSpec((1, 2 * DFT_P, tn), lambda j, b, m: (b, m, j)),
        out_shape=jax.ShapeDtypeStruct((bsz, 2 * n, c), BF16),
        compiler_params=pltpu.CompilerParams(dimension_semantics=("parallel", "parallel", "arbitrary")),
        name="hyena_conv_fwd",
    )(fwd, v_arr, spec)


def _conv_inv_kernel(g_ref, y_ref, gate_ref, zin_ref, bias_ref, *rest):
    o_ref = rest[-1]
    y = jnp.dot(g_ref[...], y_ref[0], preferred_element_type=F32)
    o_ref[0] = (gate_ref[0].astype(F32) * (y + zin_ref[0].astype(F32) * bias_ref[0])).astype(BF16)


def _conv_inv(inv, y, gate_arr, gate_col, zin_arr, zin_col, bias, order, n, row0, seq, prev_out):
    bsz, _, c = y.shape
    tn = 512
    tm = DFT_P
    r0 = row0 // tm
    in_specs = [pl.BlockSpec((tm, 2 * n), lambda j, b, m: (m, 0)),
                pl.BlockSpec((1, 2 * n, tn), lambda j, b, m: (b, 0, j)),
                pl.BlockSpec((1, tm, tn), lambda j, b, m: (b, r0 + m, gate_col + j)),
                pl.BlockSpec((1, tm, tn), lambda j, b, m: (b, r0 + m, zin_col + j)),
                pl.BlockSpec((1, 1, tn), lambda j, b, m: (order, 0, j))]
    args = [inv, y, gate_arr, zin_arr, bias.reshape(2, 1, c)]
    aliases = {}
    if prev_out is not None:
        in_specs.append(pl.BlockSpec(memory_space=pl.ANY))
        args.append(prev_out)
        aliases = {5: 0}
    return pl.pallas_call(
        _conv_inv_kernel,
        grid=(c // tn, bsz, n // tm),
        in_specs=in_specs,
        out_specs=pl.BlockSpec((1, tm, tn), lambda j, b, m: (b, r0 + m, j)),
        out_shape=jax.ShapeDtypeStruct((bsz, seq, c), BF16),
        input_output_aliases=aliases,
        compiler_params=pltpu.CompilerParams(dimension_semantics=("parallel", "parallel", "arbitrary")),
        name="hyena_conv_inv",
    )(*args)


def _hyena(z, sw, sb, spectra, dfts, bias, n_lat, n_ctx):
    bsz, seq, _ = z.shape
    c = bias.shape[1]
    u = _shortconv(z, sw, sb, n_lat)
    tiles = c // 512
    x1_col, x2_col, v_col = 0, tiles, 2 * tiles
    z1 = None
    out = None
    for n, row0 in ((n_lat, 0), (n_ctx, n_lat)):
        fwd, inv = dfts[n]
        spec = spectra[n]
        row_blk = row0 // n
        y = _conv_fwd(fwd, u, v_col, spec, 0, n, row_blk)
        z1 = _conv_inv(inv, y, u, x1_col, u, v_col, bias, 0, n, row0, seq, z1)
    for n, row0 in ((n_lat, 0), (n_ctx, n_lat)):
        fwd, inv = dfts[n]
        spec = spectra[n]
        row_blk = row0 // n
        y = _conv_fwd(fwd, z1, 0, spec, 1, n, row_blk)
        out = _conv_inv(inv, y, u, x2_col, z1, 0, bias, 1, n, row0, seq, out)
    return out


def _merge_kernel(x_ref, att_ref, hg_ref, hy_ref, ga_ref, gh_ref, gy_ref, gl_ref, gc_ref, wb_ref, wo_ref,
                  o_ref, *, n_lat, tm):
    def branch(gate_ref, val_ref, k):
        gate = jax.nn.sigmoid(gate_ref[0].astype(F32))
        return gate * jnp.dot(val_ref[0], wb_ref[k], preferred_element_type=F32)

    merged = branch(ga_ref, att_ref, 0) + branch(gh_ref, hg_ref, 1) + branch(gy_ref, hy_ref, 2)
    r = jnp.dot(merged.astype(BF16), wo_ref[...], preferred_element_type=F32)
    row = pl.program_id(1) * tm + lax.broadcasted_iota(jnp.int32, (tm, 1), 0)
    gate = jnp.where(row >= n_lat, gc_ref[0, 0], gl_ref[0, 0])
    o_ref[0] = x_ref[0] + gate * r


def _merge(x, att, hg, hy, z, mod, layer, ctx_row, wb_bf, wo_bf, n_lat):
    bsz, seq, d = x.shape
    tm = _pick(seq, (384, 256))
    tok = lambda: pl.BlockSpec((1, tm, d), lambda b, t: (b, t, 0))
    gate = lambda k: pl.BlockSpec((1, tm, d), lambda b, t: (b, t, COL_GATES * LANES // d + k))
    return pl.pallas_call(
        functools.partial(_merge_kernel, n_lat=n_lat, tm=tm),
        grid=(bsz, seq // tm),
        in_specs=[tok(), tok(), tok(), tok(), gate(0), gate(1), gate(2),
                  pl.BlockSpec((1, 1, 1, d), lambda b, t: (layer, b, 0, 2)),
                  pl.BlockSpec((1, 1, 1, d), lambda b, t: (layer, ctx_row, 0, 2)),
                  pl.BlockSpec((3, d, d), lambda b, t: (0, 0, 0)),
                  pl.BlockSpec((d, d), lambda b, t: (0, 0))],
        out_specs=tok(),
        out_shape=jax.ShapeDtypeStruct((bsz, seq, d), F32),
        compiler_params=pltpu.CompilerParams(dimension_semantics=("parallel", "parallel")),
        name="merge",
    )(x, att, hg, hy, z, z, z, mod, mod, wb_bf, wo_bf)


def _mlp_kernel(x_ref, shl_ref, scl_ref, shc_ref, scc_ref, gl_ref, gc_ref, g_ref, w1_ref, w2_ref, o_ref,
                h_ref, acc_ref, *, n_lat, tm):
    t = pl.program_id(1)
    j = pl.program_id(2)

    @pl.when(j == 0)
    def _():
        h = _modulated_norm(x_ref[0], g_ref[...], shl_ref[0, 0], scl_ref[0, 0], shc_ref[0, 0], scc_ref[0, 0],
                            t * tm, n_lat)
        h_ref[...] = h.astype(BF16)
        acc_ref[...] = jnp.zeros_like(acc_ref)

    a = jnp.maximum(jnp.dot(h_ref[...], w1_ref[...], preferred_element_type=F32), 0.0)
    acc_ref[...] += jnp.dot((a * a).astype(BF16), w2_ref[...], preferred_element_type=F32)

    @pl.when(j == pl.num_programs(2) - 1)
    def _():
        row = t * tm + lax.broadcasted_iota(jnp.int32, (tm, 1), 0)
        gate = jnp.where(row >= n_lat, gc_ref[0, 0], gl_ref[0, 0])
        o_ref[0] = x_ref[0] + gate * acc_ref[...]


def _mlp(x, mod, layer, ctx_row, g, w1_bf, w2_bf, n_lat):
    bsz, seq, d = x.shape
    ff = w1_bf.shape[1]
    tm = _pick(seq, (768, 512, 256))
    tf = _pick(ff, (1024, 512, 128))
    return pl.pallas_call(
        functools.partial(_mlp_kernel, n_lat=n_lat, tm=tm),
        grid=(bsz, seq // tm, ff // tf),
        in_specs=[pl.BlockSpec((1, tm, d), lambda b, t, j: (b, t, 0))]
        + _mod_specs(layer, ctx_row, 3, 4, d, 3)
        + [pl.BlockSpec((1, 1, 1, d), lambda b, t, j: (layer, b, 0, 5)),
           pl.BlockSpec((1, 1, 1, d), lambda b, t, j: (layer, ctx_row, 0, 5)),
           pl.BlockSpec((1, d), lambda b, t, j: (0, 0)),
           pl.BlockSpec((d, tf), lambda b, t, j: (0, j)),
           pl.BlockSpec((tf, d), lambda b, t, j: (j, 0))],
        out_specs=pl.BlockSpec((1, tm, d), lambda b, t, j: (b, t, 0)),
        out_shape=jax.ShapeDtypeStruct((bsz, seq, d), F32),
        scratch_shapes=[pltpu.VMEM((tm, d), BF16), pltpu.VMEM((tm, d), F32)],
        compiler_params=pltpu.CompilerParams(dimension_semantics=("parallel", "parallel", "arbitrary")),
        name="mlp",
    )(x, mod, mod, mod, mod, mod, mod, g.reshape(1, d), w1_bf, w2_bf)


def kernel(x, c, ctx, c_ctx, ada_w, ada_b, norm1_g, norm2_g, w_in, q_norm_g, k_norm_g, hg_lb_raw, hg_norm_g,
           hy_short_w, hy_short_b, hy_filt_w1, hy_filt_b1, hy_filt_w2, hy_filt_b2, hy_filt_w3, hy_freq, hy_bias,
           w_branch, w_out, w_mlp1, w_mlp2):
    bsz, n_lat, d = x.shape
    n_ctx = ctx.shape[1]
    depth = ada_w.shape[0]
    gate_cols = 3 * d

    cvec = jnp.concatenate([c, c_ctx[None, :]], axis=0)
    ctx_row = bsz
    pad_rows = -cvec.shape[0] % 8
    cvec = jnp.pad(cvec, ((0, pad_rows), (0, 0)))
    mod = _adaln(cvec, ada_w, ada_b)
    mod = mod.reshape(depth, cvec.shape[0], 1, 6 * d)

    cos, sin_signed = _rope_tables(n_lat, n_ctx)
    dfts = {n: _dft_matrices(n) for n in (n_lat, n_ctx)}

    xs = jnp.concatenate([x, ctx], axis=1)
    for l in range(depth):
        need_ctx = l < depth - 1
        w_l = w_in[l]
        w_bf = jnp.concatenate([w_l[:, -gate_cols:], w_l[:, :-gate_cols]], axis=1).astype(BF16)
        z = _inproj(xs, mod, l, ctx_row, norm1_g[l], w_bf, n_lat)

        att = _attention(z, cos, sin_signed, q_norm_g[l], k_norm_g[l], n_lat, True)
        hg = _hgrn(z, hg_lb_raw, hg_norm_g[l], l, n_lat, n_ctx)
        w1p = jnp.pad(hy_filt_w1[l], ((0, LANES - HY_EMB), (0, 0)))
        spectra = {n: _hyena_filters(n, w1p, hy_filt_b1[l], hy_filt_w2[l], hy_filt_b2[l], hy_filt_w3[l],
                                     hy_freq[l]) for n in (n_lat, n_ctx)}
        hy = _hyena(z, hy_short_w[l], hy_short_b[l], spectra, dfts, hy_bias[l], n_lat, n_ctx)
        xs = _merge(xs, att, hg, hy, z, mod, l, ctx_row, w_branch[l].astype(BF16), w_out[l].astype(BF16), n_lat)
        xs = _mlp(xs, mod, l, ctx_row, norm2_g[l], w_mlp1[l].astype(BF16), w_mlp2[l].astype(BF16), n_lat)
        del need_ctx
    return xs[:, :n_lat]
```

```python
import functools
import math

import jax
import jax.numpy as jnp
import numpy as np
from jax import lax
from jax.experimental import pallas as pl
from jax.experimental.pallas import tpu as pltpu

F32 = jnp.float32
BF16 = jnp.bfloat16

LANES = 128
HEAD_DIM = 128
ATT_HEADS = 8
ATT_KV_HEADS = 2
ATT_GROUP = ATT_HEADS // ATT_KV_HEADS
ROPE_THETA = 10000.0
GRID_W = 64
HG_HEADS = 8
HG_CHUNK = 64
HG_MIN_F = 1e-6
HG_HEADS_PER_STEP = 4
HY_BANDS = 16
HY_EMB = 1 + 2 * HY_BANDS
HY_FAST_DECAY = 0.3
HY_SLOW_DECAY = 1.5
HY_TARGET = 1e-2
HY_MIN_DECAY = math.log(HY_TARGET) / HY_SLOW_DECAY
HY_MAX_DECAY = math.log(HY_TARGET) / HY_FAST_DECAY
EPS = 1e-6
DFT_P = 256
HY_FWD_ROWS = 1024
HY_INV_ROWS = 512
HIGHEST = lax.Precision.HIGHEST

COL_GATES = 0
COL_ATT_Q = 24
COL_ATT_K = 32
COL_ATT_V = 34
COL_HG_Q = 36
COL_HG_F = 44
COL_HG_B = 52
COL_HG_I = 60
COL_HG_G = 68
COL_HY = 76
N_COLS = 100


def _pick(n, candidates):
    for c in candidates:
        if n % c == 0:
            return c
    raise ValueError(f"no tile in {candidates} divides {n}")


def _nt_dot(a, b):
    return lax.dot_general(a, b, (((1,), (1,)), ((), ())), preferred_element_type=F32)


def _adaln_kernel(c_ref, w_ref, b_ref, o_ref):
    cv = c_ref[...]
    s = cv * jax.nn.sigmoid(cv)
    o_ref[0] = jnp.dot(s, w_ref[0], precision=HIGHEST, preferred_element_type=F32) + b_ref[0]


def _adaln(cvec, ada_w, ada_b):
    depth, d, n6 = ada_w.shape
    rows = cvec.shape[0]
    tn = _pick(n6, (1536, 1024, 512, 128))
    return pl.pallas_call(
        _adaln_kernel,
        grid=(depth, n6 // tn),
        in_specs=[pl.BlockSpec((rows, d), lambda l, j: (0, 0)),
                  pl.BlockSpec((1, d, tn), lambda l, j: (l, 0, j)),
                  pl.BlockSpec((1, 1, tn), lambda l, j: (l, 0, j))],
        out_specs=pl.BlockSpec((1, rows, tn), lambda l, j: (l, 0, j)),
        out_shape=jax.ShapeDtypeStruct((depth, rows, n6), F32),
        name="adaln",
    )(cvec, ada_w, ada_b.reshape(depth, 1, n6))


def _modulated_norm(x, g, sh_l, sc_l, sh_c, sc_c, row0, n_lat):
    y = x * lax.rsqrt(jnp.mean(x * x, axis=-1, keepdims=True) + EPS) * g
    row = row0 + lax.broadcasted_iota(jnp.int32, (x.shape[0], 1), 0)
    is_ctx = row >= n_lat
    sc = jnp.where(is_ctx, sc_c, sc_l)
    sh = jnp.where(is_ctx, sh_c, sh_l)
    return y * (1.0 + sc) + sh


def _mod_specs(layer, ctx_row, k_shift, k_scale, d, nargs):
    def spec(row_fn, k):
        if nargs == 3:
            return pl.BlockSpec((1, 1, 1, d), lambda b, t, j: (layer, row_fn(b), 0, k))
        return pl.BlockSpec((1, 1, 1, d), lambda b, t: (layer, row_fn(b), 0, k))
    lat = lambda b: b
    ctx = lambda b: ctx_row
    return [spec(lat, k_shift), spec(lat, k_scale), spec(ctx, k_shift), spec(ctx, k_scale)]


def _inproj_kernel(x_ref, shl_ref, scl_ref, shc_ref, scc_ref, g_ref, w_ref, o_ref, h_ref, *, n_lat, tm):
    t = pl.program_id(1)

    @pl.when(pl.program_id(2) == 0)
    def _():
        h = _modulated_norm(x_ref[0], g_ref[...], shl_ref[0, 0], scl_ref[0, 0], shc_ref[0, 0], scc_ref[0, 0],
                            t * tm, n_lat)
        h_ref[...] = h.astype(BF16)

    o_ref[0] = jnp.dot(h_ref[...], w_ref[...], preferred_element_type=F32).astype(BF16)


def _inproj(x, mod, layer, ctx_row, g, w_bf, n_lat):
    bsz, seq, d = x.shape
    n = w_bf.shape[1]
    tm = _pick(seq, (768, 512, 256))
    tn = _pick(n, (2560, 1280, 640, 128))
    return pl.pallas_call(
        functools.partial(_inproj_kernel, n_lat=n_lat, tm=tm),
        grid=(bsz, seq // tm, n // tn),
        in_specs=[pl.BlockSpec((1, tm, d), lambda b, t, j: (b, t, 0))]
        + _mod_specs(layer, ctx_row, 0, 1, d, 3)
        + [pl.BlockSpec((1, d), lambda b, t, j: (0, 0)),
           pl.BlockSpec((d, tn), lambda b, t, j: (0, j))],
        out_specs=pl.BlockSpec((1, tm, tn), lambda b, t, j: (b, t, j)),
        out_shape=jax.ShapeDtypeStruct((bsz, seq, n), BF16),
        scratch_shapes=[pltpu.VMEM((tm, d), BF16)],
        compiler_params=pltpu.CompilerParams(dimension_semantics=("parallel", "parallel", "arbitrary")),
        name="inproj",
    )(x, mod, mod, mod, mod, g.reshape(1, d), w_bf)


def _rope(x, cos, sin_signed):
    lane = lax.broadcasted_iota(jnp.int32, x.shape, 1)
    first = (lane % 64) < 32
    rx = jnp.where(first, pltpu.roll(x, 96, 1), pltpu.roll(x, 32, 1))
    return x * cos + rx * sin_signed


def _head_norm(x, g):
    return x * lax.rsqrt(jnp.mean(x * x, axis=-1, keepdims=True) + EPS) * g


def _attn_kernel(q_ref, k_ref, v_ref, cq_ref, sq_ref, ck_ref, sk_ref, gq_ref, gk_ref, o_ref, ks_ref, vs_ref,
                 *, n_lat, tq):
    qi = pl.program_id(2)

    @pl.when(qi == 0)
    def _():
        k = _head_norm(k_ref[0].astype(F32), gk_ref[...])
        ks_ref[...] = _rope(k, ck_ref[...], sk_ref[...]).astype(BF16)
        vs_ref[:, :HEAD_DIM] = v_ref[0]
        vs_ref[:, HEAD_DIM:] = jnp.ones((vs_ref.shape[0], HEAD_DIM), BF16)

    scale = HEAD_DIM ** -0.5 * math.log2(math.e)
    qall = q_ref[0].astype(F32)
    qs = []
    for g in range(ATT_GROUP):
        qg = _head_norm(qall[:, g * HEAD_DIM:(g + 1) * HEAD_DIM], gq_ref[...])
        qs.append((_rope(qg, cq_ref[...], sq_ref[...]) * scale).astype(BF16))

    def attend(keys, vals):
        ss = [_nt_dot(q, keys) for q in qs]
        ps = [jnp.exp2(s - jnp.max(s, axis=-1, keepdims=True)).astype(BF16) for s in ss]
        for g, p in enumerate(ps):
            o2 = jnp.dot(p, vals, preferred_element_type=F32)
            o = o2[:, :HEAD_DIM] / o2[:, HEAD_DIM:]
            o_ref[0, :, g * HEAD_DIM:(g + 1) * HEAD_DIM] = o.astype(BF16)

    is_lat = qi * tq < n_lat

    @pl.when(is_lat)
    def _():
        attend(ks_ref[...], vs_ref[...])

    @pl.when(jnp.logical_not(is_lat))
    def _():
        attend(ks_ref[n_lat:, :], vs_ref[n_lat:, :])


def _attention(z, cos, sin_signed, gq, gk, n_lat, need_ctx):
    bsz, seq, _ = z.shape
    tq = 256
    n_q = (seq if need_ctx else n_lat) // tq
    gw = ATT_GROUP * HEAD_DIM
    return pl.pallas_call(
        functools.partial(_attn_kernel, n_lat=n_lat, tq=tq),
        grid=(bsz, ATT_KV_HEADS, n_q),
        in_specs=[pl.BlockSpec((1, tq, gw), lambda b, h, i: (b, i, COL_ATT_Q // ATT_GROUP + h)),
                  pl.BlockSpec((1, seq, HEAD_DIM), lambda b, h, i: (b, 0, COL_ATT_K + h)),
                  pl.BlockSpec((1, seq, HEAD_DIM), lambda b, h, i: (b, 0, COL_ATT_V + h)),
                  pl.BlockSpec((tq, HEAD_DIM), lambda b, h, i: (i, 0)),
                  pl.BlockSpec((tq, HEAD_DIM), lambda b, h, i: (i, 0)),
                  pl.BlockSpec((seq, HEAD_DIM), lambda b, h, i: (0, 0)),
                  pl.BlockSpec((seq, HEAD_DIM), lambda b, h, i: (0, 0)),
                  pl.BlockSpec((1, HEAD_DIM), lambda b, h, i: (0, 0)),
                  pl.BlockSpec((1, HEAD_DIM), lambda b, h, i: (0, 0))],
        out_specs=pl.BlockSpec((1, tq, gw), lambda b, h, i: (b, i, h)),
        out_shape=jax.ShapeDtypeStruct((bsz, seq, ATT_HEADS * HEAD_DIM), BF16),
        scratch_shapes=[pltpu.VMEM((seq, HEAD_DIM), BF16), pltpu.VMEM((seq, 2 * HEAD_DIM), BF16)],
        compiler_params=pltpu.CompilerParams(dimension_semantics=("parallel", "parallel", "arbitrary")),
        name="attention",
    )(z, z, z, cos, sin_signed, cos, sin_signed, gq.reshape(1, HEAD_DIM), gk.reshape(1, HEAD_DIM))


def _rope_tables(n_lat, n_ctx):
    half = HEAD_DIM // 2
    inv = ROPE_THETA ** (-np.arange(0, half, 2, dtype=np.float64) / half)
    t = np.arange(n_lat)
    ar = (t // GRID_W)[:, None] * inv
    ac = (t % GRID_W)[:, None] * inv
    ang = np.concatenate([ar, ar, ac, ac], axis=-1)
    cos = np.concatenate([np.cos(ang), np.ones((n_ctx, HEAD_DIM))], axis=0)
    sin = np.concatenate([np.sin(ang), np.zeros((n_ctx, HEAD_DIM))], axis=0)
    sign = np.where((np.arange(HEAD_DIM) % 64) < 32, -1.0, 1.0)
    return jnp.asarray(cos, F32), jnp.asarray(sin * sign, F32)


_HG_ROWS = 6 * HG_CHUNK + 16


def _hgrn_constants():
    c = HG_CHUNK
    idx = np.arange(c)
    tri = (idx[None, :] <= idx[:, None]).astype(np.float32)

    def halving(size):
        half = size // 2
        w = np.zeros((c, c), np.float32)
        for r in range(c):
            a = (r // size) * size
            if r >= a + half:
                w[r, a + half:r + 1] = 1.0
            else:
                w[r, r + 1:a + half] = 1.0
        return w

    diag = np.zeros((c, c), np.float32)
    for r in range(c):
        mid = (r // 8) * 8 + 3
        if r <= mid:
            diag[r, r + 1:mid + 1] = -1.0
        else:
            diag[r, mid + 1:r + 1] = 1.0
    suffix = (idx[None, :] > idx[:, None]).astype(np.float32)
    blocks = [tri, halving(64), halving(32), halving(16), diag, suffix]
    total = np.ones((16, c), np.float32)
    fwd = np.concatenate(blocks + [total], axis=0)
    bwd = np.concatenate([blk[::-1, ::-1] for blk in blocks] + [total], axis=0)

    level = np.zeros((c, c), np.int32)
    t, s = idx[:, None], idx[None, :]
    for code, size in ((1, 64), (2, 32), (3, 16)):
        half = size // 2
        sel = (t // size == s // size) & (t % size >= half) & (s % size < half)
        level[sel] = code
    level[(t // 8 == s // 8) & (s <= t)] = 4
    mats = np.stack([np.tile(fwd, (1, 3)), np.tile(bwd, (1, 3))])
    levels = np.stack([level, level[::-1, ::-1]])
    return jnp.asarray(mats, BF16), jnp.asarray(levels, jnp.int32)


def _hgrn_exponents(zg, lb, mat):
    sig = jax.nn.sigmoid(zg.astype(F32))
    lf = jnp.log(jnp.maximum(lb + (1.0 - lb) * sig, HG_MIN_F))
    kk = (1.0 - lb) * (1.0 - sig)
    hi = lf.astype(BF16)
    r1 = lf - hi.astype(F32)
    mid = r1.astype(BF16)
    lo = (r1 - mid.astype(F32)).astype(BF16)
    ex = jnp.dot(mat, jnp.concatenate([hi, mid, lo], axis=0), preferred_element_type=F32)
    return kk, ex


def _hgrn_scores(q, kk, ex, st):
    c = HG_CHUNK
    qf = q.astype(F32)
    inter = _nt_dot((qf * jnp.exp(ex[0:c])).astype(BF16), st.astype(BF16))
    blocks = []
    for code in (1, 2, 3, 4):
        eq = jnp.exp(ex[code * c:(code + 1) * c])
        ek = jnp.exp(-ex[code * c:(code + 1) * c]) if code == 4 else eq
        blocks.append(_nt_dot((qf * eq).astype(BF16), (kk * ek).astype(BF16)))
    return inter, blocks


def _hgrn_output(inter, blocks, v, level):
    scores = jnp.zeros_like(blocks[0])
    for code, sc in zip((1, 2, 3, 4), blocks):
        scores = jnp.where(level == code, sc, scores)
    return inter + jnp.dot(scores.astype(BF16), v, preferred_element_type=F32)


def _hgrn_state(kk, ex, v, st):
    c = HG_CHUNK
    k_suf = (kk * jnp.exp(ex[5 * c:6 * c])).astype(BF16)
    upd = lax.dot_general(v, k_suf, (((0,), (0,)), ((), ())), preferred_element_type=F32)
    return st * jnp.exp(ex[6 * c:6 * c + 1]) + upd


def _hgrn_kernel(q_ref, zf_ref, zb_ref, i_ref, g_ref, lbraw_ref, gn_ref, mat_ref, lev_ref, o_ref,
                 acc_ref, st_ref, *, layer, n_lat, n_ctx):
    c = HG_CHUNK
    nc_lat, nc_ctx = n_lat // c, n_ctx // c
    nc = nc_lat + nc_ctx
    raw = lbraw_ref[...].astype(F32)
    ew = jnp.exp(raw - jnp.max(raw, axis=0, keepdims=True))
    sm = ew / jnp.sum(ew, axis=0, keepdims=True)
    lower = jnp.zeros_like(sm[0])
    for dpt in range(1, layer + 1):
        lower = lower + sm[dpt]

    acc_ref[...] = jnp.zeros_like(acc_ref)
    st_ref[...] = jnp.zeros_like(st_ref)

    def step(n, carry):
        cf = jnp.where(n < nc_ctx, nc_lat + n, n - nc_ctx)
        cb = nc - 1 - n
        chains = []
        for hh in range(HG_HEADS_PER_STEP):
            lanes = slice(hh * LANES, (hh + 1) * LANES)
            for d, (chunk, z_ref) in enumerate(((cf, zf_ref), (cb, zb_ref))):
                rows = pl.ds(pl.multiple_of(chunk * c, c), c)
                chains.append((hh, d, rows, lanes, z_ref))
        gates = [_hgrn_exponents(z_ref[0, rows, lanes], lower[hh, d:d + 1], mat_ref[d])
                 for hh, d, rows, lanes, z_ref in chains]
        scored = [_hgrn_scores(q_ref[0, rows, lanes], kk, ex, st_ref[hh, d])
                  for (hh, d, rows, lanes, _), (kk, ex) in zip(chains, gates)]
        for (hh, d, rows, lanes, _), (inter, blocks) in zip(chains, scored):
            out = _hgrn_output(inter, blocks, i_ref[0, rows, lanes], lev_ref[d])
            acc_ref[rows, lanes] = acc_ref[rows, lanes] + out
        for (hh, d, rows, lanes, _), (kk, ex) in zip(chains, gates):
            st_ref[hh, d] = _hgrn_state(kk, ex, i_ref[0, rows, lanes], st_ref[hh, d])
        return carry

    lax.fori_loop(0, nc, step, 0)

    for hh in range(HG_HEADS_PER_STEP):
        lanes = slice(hh * LANES, (hh + 1) * LANES)
        o = acc_ref[:, lanes]
        y = o * lax.rsqrt(jnp.mean(o * o, axis=-1, keepdims=True) + EPS) * gn_ref[...]
        g = g_ref[0, :, lanes].astype(F32)
        o_ref[0, :, lanes] = (y * (g * jax.nn.sigmoid(g))).astype(BF16)


def _hgrn(z, lb_raw, g_norm, layer, n_lat, n_ctx):
    bsz, seq, _ = z.shape
    depth = lb_raw.shape[0]
    hps = HG_HEADS_PER_STEP
    width = hps * LANES
    mats, levels = _hgrn_constants()
    lbr = lb_raw.reshape(depth, 2, HG_HEADS, LANES).transpose(0, 2, 1, 3)
    col = lambda c0: pl.BlockSpec((1, seq, width), lambda b, h: (b, 0, c0 // hps + h))
    return pl.pallas_call(
        functools.partial(_hgrn_kernel, layer=layer, n_lat=n_lat, n_ctx=n_ctx),
        grid=(bsz, HG_HEADS // hps),
        in_specs=[col(COL_HG_Q), col(COL_HG_F), col(COL_HG_B), col(COL_HG_I), col(COL_HG_G),
                  pl.BlockSpec((depth, hps, 2, LANES), lambda b, h: (0, h, 0, 0)),
                  pl.BlockSpec((1, LANES), lambda b, h: (0, 0)),
                  pl.BlockSpec((2, _HG_ROWS, 3 * HG_CHUNK), lambda b, h: (0, 0, 0)),
                  pl.BlockSpec((2, HG_CHUNK, HG_CHUNK), lambda b, h: (0, 0, 0))],
        out_specs=pl.BlockSpec((1, seq, width), lambda b, h: (b, 0, h)),
        out_shape=jax.ShapeDtypeStruct((bsz, seq, HG_HEADS * LANES), BF16),
        scratch_shapes=[pltpu.VMEM((seq, width), F32), pltpu.VMEM((hps, 2, LANES, LANES), F32)],
        compiler_params=pltpu.CompilerParams(dimension_semantics=("parallel", "parallel")),
        name="hgrn",
    )(z, z, z, z, z, lbr, g_norm.reshape(1, LANES), mats, levels)


def _dft_matrices(n):
    size = 2 * n
    k = np.arange(n)[:, None]
    t = np.arange(n)[None, :]
    ang = (2.0 * np.pi / size) * ((k * t) % size)
    cos, sin = np.cos(ang), np.sin(ang)
    sin[0, :] = np.where(np.arange(n) % 2 == 0, 1.0, -1.0)
    fwd = np.concatenate([cos.reshape(n // DFT_P, DFT_P, n), sin.reshape(n // DFT_P, DFT_P, n)], axis=1)
    fwd = fwd.reshape(size, n)
    return jnp.asarray(fwd, BF16), jnp.asarray(fwd.T.copy(), BF16)


def _hyena_positions(n):
    t = np.linspace(0.0, 1.0, n)[:, None]
    w = (2.0 * math.pi / n) * np.arange(n)[:, None]
    f = np.linspace(1e-4, HY_BANDS - 1, HY_BANDS)[None, :]
    z = np.concatenate([t, np.cos(f * w), -np.sin(f * w)], axis=-1)
    return jnp.asarray(np.pad(z, ((0, 0), (0, LANES - HY_EMB))), F32)


def _filter_kernel(pos_ref, w1_ref, b1_ref, w2_ref, b2_ref, fr_ref, w3f_ref, w3b_ref, dl_ref, fs_ref, fd_ref):
    pos = pos_ref[...]
    h = jnp.sin(fr_ref[0:1] * (jnp.dot(pos, w1_ref[...], precision=HIGHEST, preferred_element_type=F32)
                               + b1_ref[...]))
    h = jnp.sin(fr_ref[1:2] * (jnp.dot(h, w2_ref[...], precision=HIGHEST, preferred_element_type=F32)
                               + b2_ref[...]))
    window = jnp.exp(-pos[:, 0:1] * dl_ref[...])
    hf = jnp.dot(h, w3f_ref[...], precision=HIGHEST, preferred_element_type=F32) * window
    hb = jnp.dot(h, w3b_ref[...], precision=HIGHEST, preferred_element_type=F32) * window
    row = lax.broadcasted_iota(jnp.int32, (pos.shape[0], 1), 0)
    hb = jnp.where(row == 0, 0.0, hb)
    norm = jnp.sum(jnp.abs(hf), axis=0, keepdims=True) + jnp.sum(jnp.abs(hb), axis=0, keepdims=True) + EPS
    inv = 1.0 / norm
    fs_ref[0] = ((hf + hb) * inv).astype(BF16)
    fd_ref[0] = ((hf - hb) * inv).astype(BF16)


def _filter_dft_kernel(a_ref, fs_ref, fd_ref, o_ref, *, n):
    m = pl.program_id(2)
    p = DFT_P
    hc = jnp.dot(a_ref[:p], fs_ref[0], preferred_element_type=F32)
    hs = jnp.dot(a_ref[p:], fd_ref[0], preferred_element_type=F32)
    k = m * p + lax.broadcasted_iota(jnp.int32, (p, 1), 0)
    weight = jnp.where(k == 0, 1.0, 2.0) / (2 * n)
    o_ref[0, :p] = hc * weight
    o_ref[0, p:] = hs * weight

    @pl.when(m == 0)
    def _():
        hny = jnp.dot(a_ref[p:p + 16], fs_ref[0], preferred_element_type=F32)[:8]
        first = lax.broadcasted_iota(jnp.int32, (8, 1), 0) == 0
        o_ref[0, p:p + 8] = jnp.where(first, hny / (2 * n), hs[:8] * weight[:8])


def _hyena_filters(n, w1p, b1, w2, b2, w3, freq):
    hid = w2.shape[0]
    c = w3.shape[1] // 4
    tn = _pick(c, (512, 256, 128))
    nct = c // tn
    pos = _hyena_positions(n)
    deltas = jnp.asarray(np.abs(np.linspace(HY_MIN_DECAY, HY_MAX_DECAY, c))[None, :], F32)
    full = lambda shape: pl.BlockSpec(shape, lambda o, j: (0,) * len(shape))
    fsum, fdiff = pl.pallas_call(
        _filter_kernel,
        grid=(2, nct),
        in_specs=[full((n, LANES)), full((LANES, hid)), full((1, hid)), full((hid, hid)), full((1, hid)),
                  full((2, hid)),
                  pl.BlockSpec((hid, tn), lambda o, j: (0, (2 * o) * nct + j)),
                  pl.BlockSpec((hid, tn), lambda o, j: (0, (2 * o + 1) * nct + j)),
                  pl.BlockSpec((1, tn), lambda o, j: (0, j))],
        out_specs=[pl.BlockSpec((1, n, tn), lambda o, j: (o, 0, j))] * 2,
        out_shape=[jax.ShapeDtypeStruct((2, n, c), BF16)] * 2,
        name="hyena_filter",
    )(pos, w1p, b1.reshape(1, hid), w2, b2.reshape(1, hid), freq, w3, w3, deltas)
    fwd, _ = _dft_matrices(n)
    return pl.pallas_call(
        functools.partial(_filter_dft_kernel, n=n),
        grid=(2, nct, n // DFT_P),
        in_specs=[pl.BlockSpec((2 * DFT_P, n), lambda o, j, m: (m, 0)),
                  pl.BlockSpec((1, n, tn), lambda o, j, m: (o, 0, j)),
                  pl.BlockSpec((1, n, tn), lambda o, j, m: (o, 0, j))],
        out_specs=pl.BlockSpec((1, 2 * DFT_P, tn), lambda o, j, m: (o, m, j)),
        out_shape=jax.ShapeDtypeStruct((2, 2 * n, c), F32),
        name="hyena_filter_dft",
    )(fwd, fsum, fdiff)


def _shortconv_kernel(u_ref, w_ref, b_ref, o_ref, *, n_lat):
    u = u_ref[0].astype(F32)
    seq = u.shape[0]
    row = lax.broadcasted_iota(jnp.int32, (seq, 1), 0)
    prev = jnp.where((row == 0) | (row == n_lat), 0.0, pltpu.roll(u, 1, 0))
    nxt = jnp.where((row == n_lat - 1) | (row == seq - 1), 0.0, pltpu.roll(u, seq - 1, 0))
    o_ref[0] = (prev * w_ref[0:1] + u * w_ref[1:2] + nxt * w_ref[2:3] + b_ref[...]).astype(BF16)


def _shortconv(z, w, b, n_lat):
    bsz, seq, _ = z.shape
    width = w.shape[1]
    tn = 256
    c0 = COL_HY * LANES // tn
    return pl.pallas_call(
        functools.partial(_shortconv_kernel, n_lat=n_lat),
        grid=(bsz, width // tn),
        in_specs=[pl.BlockSpec((1, seq, tn), lambda bb, j: (bb, 0, c0 + j)),
                  pl.BlockSpec((3, tn), lambda bb, j: (0, j)),
                  pl.BlockSpec((1, tn), lambda bb, j: (0, j))],
        out_specs=pl.BlockSpec((1, seq, tn), lambda bb, j: (bb, 0, j)),
        out_shape=jax.ShapeDtypeStruct((bsz, seq, width), BF16),
        compiler_params=pltpu.CompilerParams(dimension_semantics=("parallel", "parallel")),
        name="hyena_shortconv",
    )(z, w, b.reshape(1, width))


def _conv_fwd_kernel(a_ref, v_ref, h_ref, y_ref, *, chunks):
    p = DFT_P
    zf = jnp.dot(a_ref[...], v_ref[0], preferred_element_type=F32)
    for k in range(chunks):
        lo = 2 * p * k
        zc, zs = zf[lo:lo + p], zf[lo + p:lo + 2 * p]
        hc, hs = h_ref[0, lo:lo + p], h_ref[0, lo + p:lo + 2 * p]
        yc, ys = zc * hc - zs * hs, zc * hs + zs * hc
        if k == 0:
            real_row = (lax.broadcasted_iota(jnp.int32, (p, 1), 0) == 0) & (pl.program_id(1) == 0)
            yc = jnp.where(real_row, zc * hc, yc)
            ys = jnp.where(real_row, zs * hs, ys)
        y_ref[0, lo:lo + p] = yc.astype(BF16)
        y_ref[0, lo + p:lo + 2 * p] = ys.astype(BF16)


def _conv_fwd(fwd, v_arr, v_col, spec, order, n, row_blk):
    bsz = v_arr.shape[0]
    c = spec.shape[2]
    rows = min(HY_FWD_ROWS, 2 * n)
    return pl.pallas_call(
        functools.partial(_conv_fwd_kernel, chunks=rows // (2 * DFT_P)),
        grid=(bsz, 2 * n // rows),
        in_specs=[pl.BlockSpec((rows, n), lambda b, m: (m, 0)),
                  pl.BlockSpec((1, n, c), lambda b, m: (b, row_blk, v_col)),
                  pl.BlockSpec((1, rows, c), lambda b, m: (order, m, 0))],
        out_specs=pl.BlockSpec((1, rows, c), lambda b, m: (b, m, 0)),
        out_shape=jax.ShapeDtypeStruct((bsz, 2 * n, c), BF16),
        compiler_params=pltpu.CompilerParams(dimension_semantics=("parallel", "arbitrary")),
        name="hyena_conv_fwd",
    )(fwd, v_arr, spec)


def _conv_inv_kernel(g_ref, y_ref, gate_ref, zin_ref, bias_ref, *rest):
    o_ref = rest[-1]
    y = jnp.dot(g_ref[...], y_ref[0], preferred_element_type=F32)
    o_ref[0] = (gate_ref[0].astype(F32) * (y + zin_ref[0].astype(F32) * bias_ref[0])).astype(BF16)


def _conv_inv(inv, y, gate_arr, gate_col, zin_arr, zin_col, bias, order, n, row0, seq, prev_out):
    bsz, _, c = y.shape
    tm = min(HY_INV_ROWS, n)
    r0 = row0 // tm
    in_specs = [pl.BlockSpec((tm, 2 * n), lambda b, m: (m, 0)),
                pl.BlockSpec((1, 2 * n, c), lambda b, m: (b, 0, 0)),
                pl.BlockSpec((1, tm, c), lambda b, m: (b, r0 + m, gate_col)),
                pl.BlockSpec((1, tm, c), lambda b, m: (b, r0 + m, zin_col)),
                pl.BlockSpec((1, 1, c), lambda b, m: (order, 0, 0))]
    args = [inv, y, gate_arr, zin_arr, bias.reshape(2, 1, c)]
    aliases = {}
    if prev_out is not None:
        in_specs.append(pl.BlockSpec(memory_space=pl.ANY))
        args.append(prev_out)
        aliases = {5: 0}
    return pl.pallas_call(
        _conv_inv_kernel,
        grid=(bsz, n // tm),
        in_specs=in_specs,
        out_specs=pl.BlockSpec((1, tm, c), lambda b, m: (b, r0 + m, 0)),
        out_shape=jax.ShapeDtypeStruct((bsz, seq, c), BF16),
        input_output_aliases=aliases,
        compiler_params=pltpu.CompilerParams(dimension_semantics=("parallel", "arbitrary")),
        name="hyena_conv_inv",
    )(*args)


def _hyena(z, sw, sb, spectra, dfts, bias, n_lat, n_ctx, need_ctx):
    bsz, seq, _ = z.shape
    u = _shortconv(z, sw, sb, n_lat)
    x1_col, x2_col, v_col = 0, 1, 2
    parts = ((n_lat, 0), (n_ctx, n_lat)) if need_ctx else ((n_lat, 0),)
    z1 = None
    out = None
    for n, row0 in parts:
        fwd, inv = dfts[n]
        y = _conv_fwd(fwd, u, v_col, spectra[n], 0, n, row0 // n)
        z1 = _conv_inv(inv, y, u, x1_col, u, v_col, bias, 0, n, row0, seq, z1)
    for n, row0 in parts:
        fwd, inv = dfts[n]
        y = _conv_fwd(fwd, z1, 0, spectra[n], 1, n, row0 // n)
        out = _conv_inv(inv, y, u, x2_col, z1, 0, bias, 1, n, row0, seq, out)
    return out


def _merge_kernel(x_ref, att_ref, hg_ref, hy_ref, ga_ref, gh_ref, gy_ref, gl_ref, gc_ref, wb_ref, wo_ref,
                  o_ref, *, n_lat, tm):
    def branch(gate_ref, val_ref, k):
        gate = jax.nn.sigmoid(gate_ref[0].astype(F32))
        return gate * jnp.dot(val_ref[0], wb_ref[k], preferred_element_type=F32)

    merged = branch(ga_ref, att_ref, 0) + branch(gh_ref, hg_ref, 1) + branch(gy_ref, hy_ref, 2)
    r = jnp.dot(merged.astype(BF16), wo_ref[...], preferred_element_type=F32)
    row = pl.program_id(1) * tm + lax.broadcasted_iota(jnp.int32, (tm, 1), 0)
    gate = jnp.where(row >= n_lat, gc_ref[0, 0], gl_ref[0, 0])
    o_ref[0] = x_ref[0] + gate * r


def _merge(x, att, hg, hy, z, mod, layer, ctx_row, wb_bf, wo_bf, n_lat, rows):
    bsz, _, d = x.shape
    seq = rows
    tm = _pick(seq, (512, 384, 256))
    tok = lambda: pl.BlockSpec((1, tm, d), lambda b, t: (b, t, 0))
    gate = lambda k: pl.BlockSpec((1, tm, d), lambda b, t: (b, t, COL_GATES * LANES // d + k))
    return pl.pallas_call(
        functools.partial(_merge_kernel, n_lat=n_lat, tm=tm),
        grid=(bsz, seq // tm),
        in_specs=[tok(), tok(), tok(), tok(), gate(0), gate(1), gate(2),
                  pl.BlockSpec((1, 1, 1, d), lambda b, t: (layer, b, 0, 2)),
                  pl.BlockSpec((1, 1, 1, d), lambda b, t: (layer, ctx_row, 0, 2)),
                  pl.BlockSpec((3, d, d), lambda b, t: (0, 0, 0)),
                  pl.BlockSpec((d, d), lambda b, t: (0, 0))],
        out_specs=tok(),
        out_shape=jax.ShapeDtypeStruct((bsz, seq, d), F32),
        compiler_params=pltpu.CompilerParams(dimension_semantics=("parallel", "parallel")),
        name="merge",
    )(x, att, hg, hy, z, z, z, mod, mod, wb_bf, wo_bf)


def _mlp_kernel(x_ref, shl_ref, scl_ref, shc_ref, scc_ref, gl_ref, gc_ref, g_ref, w1_ref, w2_ref, o_ref,
                h_ref, acc_ref, *, n_lat, tm):
    t = pl.program_id(1)
    j = pl.program_id(2)

    @pl.when(j == 0)
    def _():
        h = _modulated_norm(x_ref[0], g_ref[...], shl_ref[0, 0], scl_ref[0, 0], shc_ref[0, 0], scc_ref[0, 0],
                            t * tm, n_lat)
        h_ref[...] = h.astype(BF16)
        acc_ref[...] = jnp.zeros_like(acc_ref)

    a = jnp.maximum(jnp.dot(h_ref[...], w1_ref[...], preferred_element_type=F32), 0.0)
    acc_ref[...] += jnp.dot((a * a).astype(BF16), w2_ref[...], preferred_element_type=F32)

    @pl.when(j == pl.num_programs(2) - 1)
    def _():
        row = t * tm + lax.broadcasted_iota(jnp.int32, (tm, 1), 0)
        gate = jnp.where(row >= n_lat, gc_ref[0, 0], gl_ref[0, 0])
        o_ref[0] = x_ref[0] + gate * acc_ref[...]


def _mlp(x, mod, layer, ctx_row, g, w1_bf, w2_bf, n_lat):
    bsz, seq, d = x.shape
    ff = w1_bf.shape[1]
    tm = _pick(seq, (768, 512, 256))
    tf = _pick(ff, (1024, 512, 128))
    return pl.pallas_call(
        functools.partial(_mlp_kernel, n_lat=n_lat, tm=tm),
        grid=(bsz, seq // tm, ff // tf),
        in_specs=[pl.BlockSpec((1, tm, d), lambda b, t, j: (b, t, 0))]
        + _mod_specs(layer, ctx_row, 3, 4, d, 3)
        + [pl.BlockSpec((1, 1, 1, d), lambda b, t, j: (layer, b, 0, 5)),
           pl.BlockSpec((1, 1, 1, d), lambda b, t, j: (layer, ctx_row, 0, 5)),
           pl.BlockSpec((1, d), lambda b, t, j: (0, 0)),
           pl.BlockSpec((d, tf), lambda b, t, j: (0, j)),
           pl.BlockSpec((tf, d), lambda b, t, j: (j, 0))],
        out_specs=pl.BlockSpec((1, tm, d), lambda b, t, j: (b, t, 0)),
        out_shape=jax.ShapeDtypeStruct((bsz, seq, d), F32),
        scratch_shapes=[pltpu.VMEM((tm, d), BF16), pltpu.VMEM((tm, d), F32)],
        compiler_params=pltpu.CompilerParams(dimension_semantics=("parallel", "parallel", "arbitrary")),
        name="mlp",
    )(x, mod, mod, mod, mod, mod, mod, g.reshape(1, d), w1_bf, w2_bf)


def kernel(x, c, ctx, c_ctx, ada_w, ada_b, norm1_g, norm2_g, w_in, q_norm_g, k_norm_g, hg_lb_raw, hg_norm_g,
           hy_short_w, hy_short_b, hy_filt_w1, hy_filt_b1, hy_filt_w2, hy_filt_b2, hy_filt_w3, hy_freq, hy_bias,
           w_branch, w_out, w_mlp1, w_mlp2):
    bsz, n_lat, d = x.shape
    n_ctx = ctx.shape[1]
    depth = ada_w.shape[0]
    gate_cols = 3 * d

    cvec = jnp.concatenate([c, c_ctx[None, :]], axis=0)
    ctx_row = bsz
    pad_rows = -cvec.shape[0] % 8
    cvec = jnp.pad(cvec, ((0, pad_rows), (0, 0)))
    mod = _adaln(cvec, ada_w, ada_b)
    mod = mod.reshape(depth, cvec.shape[0], 1, 6 * d)

    cos, sin_signed = _rope_tables(n_lat, n_ctx)
    dfts = {n: _dft_matrices(n) for n in (n_lat, n_ctx)}

    xs = jnp.concatenate([x, ctx], axis=1)
    for l in range(depth):
        need_ctx = l < depth - 1
        w_l = w_in[l]
        w_bf = jnp.concatenate([w_l[:, -gate_cols:], w_l[:, :-gate_cols]], axis=1).astype(BF16)
        z = _inproj(xs, mod, l, ctx_row, norm1_g[l], w_bf, n_lat)

        att = _attention(z, cos, sin_signed, q_norm_g[l], k_norm_g[l], n_lat, need_ctx)
        hg = _hgrn(z, hg_lb_raw, hg_norm_g[l], l, n_lat, n_ctx)
        w1p = jnp.pad(hy_filt_w1[l], ((0, LANES - HY_EMB), (0, 0)))
        spectra = {n: _hyena_filters(n, w1p, hy_filt_b1[l], hy_filt_w2[l], hy_filt_b2[l], hy_filt_w3[l],
                                     hy_freq[l]) for n in ((n_lat, n_ctx) if need_ctx else (n_lat,))}
        hy = _hyena(z, hy_short_w[l], hy_short_b[l], spectra, dfts, hy_bias[l], n_lat, n_ctx, need_ctx)
        rows = n_lat + n_ctx if need_ctx else n_lat
        xs = _merge(xs, att, hg, hy, z, mod, l, ctx_row, w_branch[l].astype(BF16), w_out[l].astype(BF16), n_lat,
                    rows)
        xs = _mlp(xs, mod, l, ctx_row, norm2_g[l], w_mlp1[l].astype(BF16), w_mlp2[l].astype(BF16), n_lat)
    return xs
```

```python
import functools
import math

import jax
import jax.numpy as jnp
import numpy as np
from jax import lax
from jax.experimental import pallas as pl
from jax.experimental.pallas import tpu as pltpu

F32 = jnp.float32
BF16 = jnp.bfloat16

LANES = 128
HEAD_DIM = 128
ATT_HEADS = 8
ATT_KV_HEADS = 2
ATT_GROUP = ATT_HEADS // ATT_KV_HEADS
ROPE_THETA = 10000.0
GRID_W = 64
HG_HEADS = 8
HG_CHUNK = 64
HG_MIN_F = 1e-6
HG_HEADS_PER_STEP = 4
HY_BANDS = 16
HY_EMB = 1 + 2 * HY_BANDS
HY_FAST_DECAY = 0.3
HY_SLOW_DECAY = 1.5
HY_TARGET = 1e-2
HY_MIN_DECAY = math.log(HY_TARGET) / HY_SLOW_DECAY
HY_MAX_DECAY = math.log(HY_TARGET) / HY_FAST_DECAY
EPS = 1e-6
DFT_P = 256
HY_FWD_ROWS = 1024
HY_INV_ROWS = 512
HIGHEST = lax.Precision.HIGHEST

COL_GATES = 0
COL_ATT_Q = 24
COL_ATT_K = 32
COL_ATT_V = 34
COL_HG_Q = 36
COL_HG_F = 44
COL_HG_B = 52
COL_HG_I = 60
COL_HG_G = 68
COL_HY = 76
N_COLS = 100


def _pick(n, candidates):
    for c in candidates:
        if n % c == 0:
            return c
    raise ValueError(f"no tile in {candidates} divides {n}")


def _nt_dot(a, b):
    return lax.dot_general(a, b, (((1,), (1,)), ((), ())), preferred_element_type=F32)


def _adaln_kernel(c_ref, w_ref, b_ref, o_ref):
    cv = c_ref[...]
    s = cv * jax.nn.sigmoid(cv)
    o_ref[0] = jnp.dot(s, w_ref[0], precision=HIGHEST, preferred_element_type=F32) + b_ref[0]


def _adaln(cvec, ada_w, ada_b):
    depth, d, n6 = ada_w.shape
    rows = cvec.shape[0]
    tn = _pick(n6, (1536, 1024, 512, 128))
    return pl.pallas_call(
        _adaln_kernel,
        grid=(depth, n6 // tn),
        in_specs=[pl.BlockSpec((rows, d), lambda l, j: (0, 0)),
                  pl.BlockSpec((1, d, tn), lambda l, j: (l, 0, j)),
                  pl.BlockSpec((1, 1, tn), lambda l, j: (l, 0, j))],
        out_specs=pl.BlockSpec((1, rows, tn), lambda l, j: (l, 0, j)),
        out_shape=jax.ShapeDtypeStruct((depth, rows, n6), F32),
        name="adaln",
    )(cvec, ada_w, ada_b.reshape(depth, 1, n6))


def _modulated_norm(x, g, sh_l, sc_l, sh_c, sc_c, row0, n_lat):
    y = x * lax.rsqrt(jnp.mean(x * x, axis=-1, keepdims=True) + EPS) * g
    row = row0 + lax.broadcasted_iota(jnp.int32, (x.shape[0], 1), 0)
    is_ctx = row >= n_lat
    sc = jnp.where(is_ctx, sc_c, sc_l)
    sh = jnp.where(is_ctx, sh_c, sh_l)
    return y * (1.0 + sc) + sh


def _mod_specs(layer, ctx_row, k_shift, k_scale, d, nargs):
    def spec(row_fn, k):
        if nargs == 3:
            return pl.BlockSpec((1, 1, 1, d), lambda b, t, j: (layer, row_fn(b), 0, k))
        return pl.BlockSpec((1, 1, 1, d), lambda b, t: (layer, row_fn(b), 0, k))
    lat = lambda b: b
    ctx = lambda b: ctx_row
    return [spec(lat, k_shift), spec(lat, k_scale), spec(ctx, k_shift), spec(ctx, k_scale)]


def _inproj_kernel(x_ref, shl_ref, scl_ref, shc_ref, scc_ref, g_ref, w_ref, o_ref, h_ref, *, n_lat, tm):
    t = pl.program_id(1)

    @pl.when(pl.program_id(2) == 0)
    def _():
        h = _modulated_norm(x_ref[0], g_ref[...], shl_ref[0, 0], scl_ref[0, 0], shc_ref[0, 0], scc_ref[0, 0],
                            t * tm, n_lat)
        h_ref[...] = h.astype(BF16)

    o_ref[0] = jnp.dot(h_ref[...], w_ref[...], preferred_element_type=F32).astype(BF16)


def _inproj(x, mod, layer, ctx_row, g, w_bf, n_lat):
    bsz, seq, d = x.shape
    n = w_bf.shape[1]
    tm = _pick(seq, (768, 512, 256))
    tn = _pick(n, (2560, 1280, 640, 128))
    return pl.pallas_call(
        functools.partial(_inproj_kernel, n_lat=n_lat, tm=tm),
        grid=(bsz, seq // tm, n // tn),
        in_specs=[pl.BlockSpec((1, tm, d), lambda b, t, j: (b, t, 0))]
        + _mod_specs(layer, ctx_row, 0, 1, d, 3)
        + [pl.BlockSpec((1, d), lambda b, t, j: (0, 0)),
           pl.BlockSpec((d, tn), lambda b, t, j: (0, j))],
        out_specs=pl.BlockSpec((1, tm, tn), lambda b, t, j: (b, t, j)),
        out_shape=jax.ShapeDtypeStruct((bsz, seq, n), BF16),
        scratch_shapes=[pltpu.VMEM((tm, d), BF16)],
        compiler_params=pltpu.CompilerParams(dimension_semantics=("parallel", "parallel", "arbitrary")),
        name="inproj",
    )(x, mod, mod, mod, mod, g.reshape(1, d), w_bf)


def _rope(x, cos, sin_signed):
    lane = lax.broadcasted_iota(jnp.int32, x.shape, 1)
    first = (lane % 64) < 32
    rx = jnp.where(first, pltpu.roll(x, 96, 1), pltpu.roll(x, 32, 1))
    return x * cos + rx * sin_signed


def _head_norm(x, g):
    return x * lax.rsqrt(jnp.mean(x * x, axis=-1, keepdims=True) + EPS) * g


def _attn_kernel(q_ref, k_ref, v_ref, cq_ref, sq_ref, ck_ref, sk_ref, gq_ref, gk_ref, o_ref, ks_ref, vs_ref,
                 *, n_lat, tq):
    qi = pl.program_id(2)

    @pl.when(qi == 0)
    def _():
        k = _head_norm(k_ref[0].astype(F32), gk_ref[...])
        ks_ref[...] = _rope(k, ck_ref[...], sk_ref[...]).astype(BF16)
        vs_ref[:, :HEAD_DIM] = v_ref[0]
        vs_ref[:, HEAD_DIM:] = jnp.ones((vs_ref.shape[0], HEAD_DIM), BF16)

    scale = HEAD_DIM ** -0.5 * math.log2(math.e)

    def attend(keys, vals):
        qall = q_ref[0].astype(F32)
        qs = []
        for g in range(ATT_GROUP):
            qg = _head_norm(qall[:, g * HEAD_DIM:(g + 1) * HEAD_DIM], gq_ref[...])
            qs.append((_rope(qg, cq_ref[...], sq_ref[...]) * scale).astype(BF16))
        ss = [_nt_dot(q, keys) for q in qs]
        ps = [jnp.exp2(s - jnp.max(s, axis=-1, keepdims=True)).astype(BF16) for s in ss]
        for g, p in enumerate(ps):
            o2 = jnp.dot(p, vals, preferred_element_type=F32)
            o = o2[:, :HEAD_DIM] / o2[:, HEAD_DIM:]
            o_ref[0, :, g * HEAD_DIM:(g + 1) * HEAD_DIM] = o.astype(BF16)

    is_lat = qi * tq < n_lat

    @pl.when(is_lat)
    def _():
        attend(ks_ref[...], vs_ref[...])

    @pl.when(jnp.logical_not(is_lat))
    def _():
        attend(ks_ref[n_lat:, :], vs_ref[n_lat:, :])


def _attention(z, cos, sin_signed, gq, gk, n_lat, need_ctx):
    bsz, seq, _ = z.shape
    tq = 256
    n_q = (seq if need_ctx else n_lat) // tq
    gw = ATT_GROUP * HEAD_DIM
    return pl.pallas_call(
        functools.partial(_attn_kernel, n_lat=n_lat, tq=tq),
        grid=(bsz, ATT_KV_HEADS, n_q),
        in_specs=[pl.BlockSpec((1, tq, gw), lambda b, h, i: (b, i, COL_ATT_Q // ATT_GROUP + h)),
                  pl.BlockSpec((1, seq, HEAD_DIM), lambda b, h, i: (b, 0, COL_ATT_K + h)),
                  pl.BlockSpec((1, seq, HEAD_DIM), lambda b, h, i: (b, 0, COL_ATT_V + h)),
                  pl.BlockSpec((tq, HEAD_DIM), lambda b, h, i: (i, 0)),
                  pl.BlockSpec((tq, HEAD_DIM), lambda b, h, i: (i, 0)),
                  pl.BlockSpec((seq, HEAD_DIM), lambda b, h, i: (0, 0)),
                  pl.BlockSpec((seq, HEAD_DIM), lambda b, h, i: (0, 0)),
                  pl.BlockSpec((1, HEAD_DIM), lambda b, h, i: (0, 0)),
                  pl.BlockSpec((1, HEAD_DIM), lambda b, h, i: (0, 0))],
        out_specs=pl.BlockSpec((1, tq, gw), lambda b, h, i: (b, i, h)),
        out_shape=jax.ShapeDtypeStruct((bsz, n_q * tq, ATT_HEADS * HEAD_DIM), BF16),
        scratch_shapes=[pltpu.VMEM((seq, HEAD_DIM), BF16), pltpu.VMEM((seq, 2 * HEAD_DIM), BF16)],
        compiler_params=pltpu.CompilerParams(dimension_semantics=("parallel", "parallel", "arbitrary")),
        name="attention",
    )(z, z, z, cos, sin_signed, cos, sin_signed, gq.reshape(1, HEAD_DIM), gk.reshape(1, HEAD_DIM))


def _rope_tables(n_lat, n_ctx):
    half = HEAD_DIM // 2
    inv = ROPE_THETA ** (-np.arange(0, half, 2, dtype=np.float64) / half)
    t = np.arange(n_lat)
    ar = (t // GRID_W)[:, None] * inv
    ac = (t % GRID_W)[:, None] * inv
    ang = np.concatenate([ar, ar, ac, ac], axis=-1)
    cos = np.concatenate([np.cos(ang), np.ones((n_ctx, HEAD_DIM))], axis=0)
    sin = np.concatenate([np.sin(ang), np.zeros((n_ctx, HEAD_DIM))], axis=0)
    sign = np.where((np.arange(HEAD_DIM) % 64) < 32, -1.0, 1.0)
    return jnp.asarray(cos, F32), jnp.asarray(sin * sign, F32)


def _hgrn_masks():
    c = HG_CHUNK
    idx = np.arange(c)
    t, s = idx[:, None], idx[None, :]

    def halving(size):
        half = size // 2
        return (t // size == s // size) & (t % size >= half) & (s % size < half)

    diag = (t // 8 == s // 8) & (s <= t)
    flip = lambda m: m[::-1, ::-1]
    fwd = np.stack([np.concatenate([halving(64), halving(32)], axis=1),
                    np.concatenate([halving(16), diag], axis=1)])
    bwd = np.stack([np.concatenate([flip(halving(64)), flip(halving(32))], axis=1),
                    np.concatenate([flip(halving(16)), flip(diag)], axis=1)])
    return jnp.asarray(np.stack([fwd, bwd]), F32)


def _hgrn_exponents(zg, lb, reverse):
    tile = 8
    nt = HG_CHUNK // tile
    sig = jax.nn.sigmoid(zg.astype(F32))
    f = lb + (1.0 - lb) * sig
    lf = jnp.log2(jnp.maximum(f, HG_MIN_F))
    kk = 1.0 - f
    p = lf.reshape(nt, tile, LANES)
    sub = lax.broadcasted_iota(jnp.int32, p.shape, 1)
    for k in (1, 2, 4):
        if reverse:
            p = p + jnp.where(sub < tile - k, pltpu.roll(p, tile - k, 1), 0.0)
        else:
            p = p + jnp.where(sub >= k, pltpu.roll(p, k, 1), 0.0)
    order = list(range(nt - 1, -1, -1)) if reverse else list(range(nt))
    tot_row, mid_row = (0, 4) if reverse else (tile - 1, 3)
    s = [p[j] for j in order]
    tb = [jnp.broadcast_to(p[j, tot_row:tot_row + 1, :], (tile, LANES)) for j in order]
    mb = [jnp.broadcast_to(p[j, mid_row:mid_row + 1, :], (tile, LANES)) for j in order]
    t01, t23, t45 = tb[0] + tb[1], tb[2] + tb[3], tb[4] + tb[5]
    t123, t456 = tb[1] + t23, t45 + tb[6]
    c4 = t01 + t23
    before = [None, tb[0], t01, t01 + tb[2], c4, c4 + tb[4], c4 + t45, c4 + t456]
    total = before[7] + tb[7]
    cum = [s[0]] + [before[k] + s[k] for k in range(1, nt)]
    suf = [total - cum[k] for k in range(nt)]
    w8 = [s[k] - mb[k] for k in range(nt)]
    w16 = [s[k] if k % 2 else tb[k] - s[k] for k in range(nt)]
    w32 = [t01 - s[0], tb[1] - s[1], s[2], tb[2] + s[3], t45 - s[4], tb[5] - s[5], s[6], tb[6] + s[7]]
    w64 = [c4 - s[0], t123 - s[1], t23 - s[2], tb[3] - s[3], s[4], tb[4] + s[5], t45 + s[6], t456 + s[7]]

    def natural(parts):
        return jnp.concatenate([parts[order.index(j)] for j in range(nt)], axis=0)

    ex = {"cum": natural(cum), "suf": natural(suf), 64: natural(w64), 32: natural(w32), 16: natural(w16),
          8: natural(w8), "tot": total[0:1]}
    return kk, ex


def _hgrn_scores(q, kk, ex, st):
    qf = q.astype(F32)
    inter = _nt_dot((qf * jnp.exp2(ex["cum"])).astype(BF16), st.astype(BF16))
    zeros = jnp.zeros(q.shape, BF16)
    pairs = []
    for la, lb_ in ((64, 32), (16, 8)):
        ea, eb = jnp.exp2(ex[la]), jnp.exp2(ex[lb_])
        eb_k = jnp.exp2(-ex[lb_]) if lb_ == 8 else eb
        qa = jnp.concatenate([(qf * ea).astype(BF16), (qf * eb).astype(BF16)], axis=1)
        ka = jnp.concatenate([(kk * ea).astype(BF16), zeros], axis=1)
        kb = jnp.concatenate([zeros, (kk * eb_k).astype(BF16)], axis=1)
        pairs.append(_nt_dot(qa, jnp.concatenate([ka, kb], axis=0)))
    return inter, pairs


def _hgrn_output(inter, pairs, v, mask):
    p0 = pairs[0] * mask[0]
    p1 = jnp.where(mask[1] > 0.0, pairs[1], 0.0)
    probs = jnp.concatenate([p0.astype(BF16), p1.astype(BF16)], axis=1)
    return inter + jnp.dot(probs, jnp.concatenate([v, v, v, v], axis=0), preferred_element_type=F32)


def _hgrn_state(kk, ex, v, st):
    k_suf = (kk * jnp.exp2(ex["suf"])).astype(BF16)
    upd = lax.dot_general(v, k_suf, (((0,), (0,)), ((), ())), preferred_element_type=F32)
    return st * jnp.exp2(ex["tot"]) + upd


def _hgrn_kernel(q_ref, zf_ref, zb_ref, i_ref, g_ref, lbraw_ref, gn_ref, mask_ref, o_ref,
                 acc_ref, st_ref, *, layer, n_lat, n_ctx):
    c = HG_CHUNK
    nc_lat, nc_ctx = n_lat // c, n_ctx // c
    nc = nc_lat + nc_ctx
    raw = lbraw_ref[...].astype(F32)
    ew = jnp.exp(raw - jnp.max(raw, axis=0, keepdims=True))
    sm = ew / jnp.sum(ew, axis=0, keepdims=True)
    lower = jnp.zeros_like(sm[0])
    for dpt in range(1, layer + 1):
        lower = lower + sm[dpt]

    acc_ref[...] = jnp.zeros_like(acc_ref)
    st_ref[...] = jnp.zeros_like(st_ref)

    def step(n, carry):
        cf = jnp.where(n < nc_ctx, nc_lat + n, n - nc_ctx)
        cb = nc - 1 - n
        chains = []
        for hh in range(HG_HEADS_PER_STEP):
            lanes = slice(hh * LANES, (hh + 1) * LANES)
            for d, (chunk, z_ref) in enumerate(((cf, zf_ref), (cb, zb_ref))):
                rows = pl.ds(pl.multiple_of(chunk * c, c), c)
                chains.append((hh, d, rows, lanes, z_ref))
        gates = [_hgrn_exponents(z_ref[0, rows, lanes], lower[hh, d:d + 1], d == 1)
                 for hh, d, rows, lanes, z_ref in chains]
        scored = [_hgrn_scores(q_ref[0, rows, lanes], kk, ex, st_ref[hh, d])
                  for (hh, d, rows, lanes, _), (kk, ex) in zip(chains, gates)]
        for (hh, d, rows, lanes, _), (inter, pairs) in zip(chains, scored):
            out = _hgrn_output(inter, pairs, i_ref[0, rows, lanes], mask_ref[d])
            acc_ref[rows, lanes] = acc_ref[rows, lanes] + out
        for (hh, d, rows, lanes, _), (kk, ex) in zip(chains, gates):
            st_ref[hh, d] = _hgrn_state(kk, ex, i_ref[0, rows, lanes], st_ref[hh, d])
        return carry

    lax.fori_loop(0, nc, step, 0)

    for hh in range(HG_HEADS_PER_STEP):
        lanes = slice(hh * LANES, (hh + 1) * LANES)
        o = acc_ref[:, lanes]
        y = o * lax.rsqrt(jnp.mean(o * o, axis=-1, keepdims=True) + EPS) * gn_ref[...]
        g = g_ref[0, :, lanes].astype(F32)
        o_ref[0, :, lanes] = (y * (g * jax.nn.sigmoid(g))).astype(BF16)


def _hgrn(z, lb_raw, g_norm, layer, n_lat, n_ctx):
    bsz, seq, _ = z.shape
    depth = lb_raw.shape[0]
    hps = HG_HEADS_PER_STEP
    width = hps * LANES
    masks = _hgrn_masks()
    lbr = lb_raw.reshape(depth, 2, HG_HEADS, LANES).transpose(0, 2, 1, 3)
    col = lambda c0: pl.BlockSpec((1, seq, width), lambda b, h: (b, 0, c0 // hps + h))
    return pl.pallas_call(
        functools.partial(_hgrn_kernel, layer=layer, n_lat=n_lat, n_ctx=n_ctx),
        grid=(bsz, HG_HEADS // hps),
        in_specs=[col(COL_HG_Q), col(COL_HG_F), col(COL_HG_B), col(COL_HG_I), col(COL_HG_G),
                  pl.BlockSpec((depth, hps, 2, LANES), lambda b, h: (0, h, 0, 0)),
                  pl.BlockSpec((1, LANES), lambda b, h: (0, 0)),
                  pl.BlockSpec((2, 2, HG_CHUNK, LANES), lambda b, h: (0, 0, 0, 0))],
        out_specs=pl.BlockSpec((1, seq, width), lambda b, h: (b, 0, h)),
        out_shape=jax.ShapeDtypeStruct((bsz, seq, HG_HEADS * LANES), BF16),
        scratch_shapes=[pltpu.VMEM((seq, width), F32), pltpu.VMEM((hps, 2, LANES, LANES), F32)],
        compiler_params=pltpu.CompilerParams(dimension_semantics=("parallel", "parallel")),
        name="hgrn",
    )(z, z, z, z, z, lbr, g_norm.reshape(1, LANES), masks)


def _dft_matrices(n):
    size = 2 * n
    k = np.arange(n)[:, None]
    t = np.arange(n)[None, :]
    ang = (2.0 * np.pi / size) * ((k * t) % size)
    cos, sin = np.cos(ang), np.sin(ang)
    sin[0, :] = np.where(np.arange(n) % 2 == 0, 1.0, -1.0)
    fwd = np.concatenate([cos.reshape(n // DFT_P, DFT_P, n), sin.reshape(n // DFT_P, DFT_P, n)], axis=1)
    fwd = fwd.reshape(size, n)
    return jnp.asarray(fwd, BF16), jnp.asarray(fwd.T.copy(), BF16)


def _hyena_positions(n):
    t = np.linspace(0.0, 1.0, n)[:, None]
    w = (2.0 * math.pi / n) * np.arange(n)[:, None]
    f = np.linspace(1e-4, HY_BANDS - 1, HY_BANDS)[None, :]
    z = np.concatenate([t, np.cos(f * w), -np.sin(f * w)], axis=-1)
    return jnp.asarray(np.pad(z, ((0, 0), (0, LANES - HY_EMB))), F32)


def _filter_kernel(pos_ref, w1_ref, b1_ref, w2_ref, b2_ref, fr_ref, w3f_ref, w3b_ref, dl_ref, fs_ref, fd_ref,
                   h_ref):
    pos = pos_ref[...]

    @pl.when((pl.program_id(0) == 0) & (pl.program_id(1) == 0))
    def _():
        h1 = jnp.sin(fr_ref[0:1] * (jnp.dot(pos, w1_ref[...], precision=HIGHEST, preferred_element_type=F32)
                                    + b1_ref[...]))
        h_ref[...] = jnp.sin(fr_ref[1:2] * (jnp.dot(h1, w2_ref[...], precision=HIGHEST,
                                                    preferred_element_type=F32) + b2_ref[...]))

    h = h_ref[...]
    window = jnp.exp(-pos[:, 0:1] * dl_ref[...])
    hf = jnp.dot(h, w3f_ref[...], precision=HIGHEST, preferred_element_type=F32) * window
    hb = jnp.dot(h, w3b_ref[...], precision=HIGHEST, preferred_element_type=F32) * window
    row = lax.broadcasted_iota(jnp.int32, (pos.shape[0], 1), 0)
    hb = jnp.where(row == 0, 0.0, hb)
    norm = jnp.sum(jnp.abs(hf), axis=0, keepdims=True) + jnp.sum(jnp.abs(hb), axis=0, keepdims=True) + EPS
    inv = 1.0 / norm
    fs_ref[0] = ((hf + hb) * inv).astype(BF16)
    fd_ref[0] = ((hf - hb) * inv).astype(BF16)


def _filter_dft_kernel(a_ref, fs_ref, fd_ref, o_ref, *, n):
    m = pl.program_id(2)
    p = DFT_P
    hc = jnp.dot(a_ref[:p], fs_ref[0], preferred_element_type=F32)
    hs = jnp.dot(a_ref[p:], fd_ref[0], preferred_element_type=F32)
    k = m * p + lax.broadcasted_iota(jnp.int32, (p, 1), 0)
    weight = jnp.where(k == 0, 1.0, 2.0) / (2 * n)
    o_ref[0, :p] = hc * weight
    o_ref[0, p:] = hs * weight

    @pl.when(m == 0)
    def _():
        hny = jnp.dot(a_ref[p:p + 16], fs_ref[0], preferred_element_type=F32)[:8]
        first = lax.broadcasted_iota(jnp.int32, (8, 1), 0) == 0
        o_ref[0, p:p + 8] = jnp.where(first, hny / (2 * n), hs[:8] * weight[:8])


def _hyena_filters(n, w1p, b1, w2, b2, w3, freq):
    hid = w2.shape[0]
    c = w3.shape[1] // 4
    tn = _pick(c, (512, 256, 128))
    nct = c // tn
    pos = _hyena_positions(n)
    deltas = jnp.asarray(np.abs(np.linspace(HY_MIN_DECAY, HY_MAX_DECAY, c))[None, :], F32)
    full = lambda shape: pl.BlockSpec(shape, lambda o, j: (0,) * len(shape))
    fsum, fdiff = pl.pallas_call(
        _filter_kernel,
        grid=(2, nct),
        in_specs=[full((n, LANES)), full((LANES, hid)), full((1, hid)), full((hid, hid)), full((1, hid)),
                  full((2, hid)),
                  pl.BlockSpec((hid, tn), lambda o, j: (0, (2 * o) * nct + j)),
                  pl.BlockSpec((hid, tn), lambda o, j: (0, (2 * o + 1) * nct + j)),
                  pl.BlockSpec((1, tn), lambda o, j: (0, j))],
        out_specs=[pl.BlockSpec((1, n, tn), lambda o, j: (o, 0, j))] * 2,
        out_shape=[jax.ShapeDtypeStruct((2, n, c), BF16)] * 2,
        scratch_shapes=[pltpu.VMEM((n, hid), F32)],
        compiler_params=pltpu.CompilerParams(dimension_semantics=("arbitrary", "arbitrary")),
        name="hyena_filter",
    )(pos, w1p, b1.reshape(1, hid), w2, b2.reshape(1, hid), freq, w3, w3, deltas)
    fwd, _ = _dft_matrices(n)
    return pl.pallas_call(
        functools.partial(_filter_dft_kernel, n=n),
        grid=(2, nct, n // DFT_P),
        in_specs=[pl.BlockSpec((2 * DFT_P, n), lambda o, j, m: (m, 0)),
                  pl.BlockSpec((1, n, tn), lambda o, j, m: (o, 0, j)),
                  pl.BlockSpec((1, n, tn), lambda o, j, m: (o, 0, j))],
        out_specs=pl.BlockSpec((1, 2 * DFT_P, tn), lambda o, j, m: (o, m, j)),
        out_shape=jax.ShapeDtypeStruct((2, 2 * n, c), F32),
        name="hyena_filter_dft",
    )(fwd, fsum, fdiff)


def _shortconv_kernel(u_ref, w_ref, b_ref, o_ref, *, n_lat):
    u = u_ref[0].astype(F32)
    seq = u.shape[0]
    row = lax.broadcasted_iota(jnp.int32, (seq, 1), 0)
    prev = jnp.where((row == 0) | (row == n_lat), 0.0, pltpu.roll(u, 1, 0))
    nxt = jnp.where((row == n_lat - 1) | (row == seq - 1), 0.0, pltpu.roll(u, seq - 1, 0))
    o_ref[0] = (prev * w_ref[0:1] + u * w_ref[1:2] + nxt * w_ref[2:3] + b_ref[...]).astype(BF16)


def _shortconv(z, w, b, n_lat):
    bsz, seq, _ = z.shape
    width = w.shape[1]
    tn = 256
    c0 = COL_HY * LANES // tn
    return pl.pallas_call(
        functools.partial(_shortconv_kernel, n_lat=n_lat),
        grid=(bsz, width // tn),
        in_specs=[pl.BlockSpec((1, seq, tn), lambda bb, j: (bb, 0, c0 + j)),
                  pl.BlockSpec((3, tn), lambda bb, j: (0, j)),
                  pl.BlockSpec((1, tn), lambda bb, j: (0, j))],
        out_specs=pl.BlockSpec((1, seq, tn), lambda bb, j: (bb, 0, j)),
        out_shape=jax.ShapeDtypeStruct((bsz, seq, width), BF16),
        compiler_params=pltpu.CompilerParams(dimension_semantics=("parallel", "parallel")),
        name="hyena_shortconv",
    )(z, w, b.reshape(1, width))


def _conv_fwd_kernel(a_ref, v_ref, h_ref, y_ref, *, chunks):
    p = DFT_P
    zf = jnp.dot(a_ref[...], v_ref[0], preferred_element_type=F32)
    for k in range(chunks):
        lo = 2 * p * k
        zc, zs = zf[lo:lo + p], zf[lo + p:lo + 2 * p]
        hc, hs = h_ref[0, lo:lo + p], h_ref[0, lo + p:lo + 2 * p]
        yc, ys = zc * hc - zs * hs, zc * hs + zs * hc
        if k == 0:
            real_row = (lax.broadcasted_iota(jnp.int32, (p, 1), 0) == 0) & (pl.program_id(1) == 0)
            yc = jnp.where(real_row, zc * hc, yc)
            ys = jnp.where(real_row, zs * hs, ys)
        y_ref[0, lo:lo + p] = yc.astype(BF16)
        y_ref[0, lo + p:lo + 2 * p] = ys.astype(BF16)


def _conv_fwd(fwd, v_arr, v_col, spec, order, n, row_blk):
    bsz = v_arr.shape[0]
    c = spec.shape[2]
    rows = min(HY_FWD_ROWS, 2 * n)
    return pl.pallas_call(
        functools.partial(_conv_fwd_kernel, chunks=rows // (2 * DFT_P)),
        grid=(bsz, 2 * n // rows),
        in_specs=[pl.BlockSpec((rows, n), lambda b, m: (m, 0)),
                  pl.BlockSpec((1, n, c), lambda b, m: (b, row_blk, v_col)),
                  pl.BlockSpec((1, rows, c), lambda b, m: (order, m, 0))],
        out_specs=pl.BlockSpec((1, rows, c), lambda b, m: (b, m, 0)),
        out_shape=jax.ShapeDtypeStruct((bsz, 2 * n, c), BF16),
        compiler_params=pltpu.CompilerParams(dimension_semantics=("parallel", "arbitrary")),
        name="hyena_conv_fwd",
    )(fwd, v_arr, spec)


def _conv_inv_kernel(g_ref, y_ref, gate_ref, zin_ref, bias_ref, o_ref):
    y = jnp.dot(g_ref[...], y_ref[0], preferred_element_type=F32)
    o_ref[0] = (gate_ref[0].astype(F32) * (y + zin_ref[0].astype(F32) * bias_ref[0])).astype(BF16)


def _conv_inv(inv, y, gate, zin, bias, order, n):
    bsz, _, c = y.shape
    tm = min(HY_INV_ROWS, n)

    def rows_of(triple):
        _, row0, col = triple
        return pl.BlockSpec((1, tm, c), lambda b, m: (b, row0 // tm + m, col))

    return pl.pallas_call(
        _conv_inv_kernel,
        grid=(bsz, n // tm),
        in_specs=[pl.BlockSpec((tm, 2 * n), lambda b, m: (m, 0)),
                  pl.BlockSpec((1, 2 * n, c), lambda b, m: (b, 0, 0)),
                  rows_of(gate), rows_of(zin),
                  pl.BlockSpec((1, 1, c), lambda b, m: (order, 0, 0))],
        out_specs=pl.BlockSpec((1, tm, c), lambda b, m: (b, m, 0)),
        out_shape=jax.ShapeDtypeStruct((bsz, n, c), BF16),
        compiler_params=pltpu.CompilerParams(dimension_semantics=("parallel", "arbitrary")),
        name="hyena_conv_inv",
    )(inv, y, gate[0], zin[0], bias.reshape(2, 1, c))


def _hyena(z, sw, sb, spectra, dfts, bias, n_lat, n_ctx, need_ctx):
    u = _shortconv(z, sw, sb, n_lat)
    x1_col, x2_col, v_col = 0, 1, 2
    outs = []
    for n, row0 in ((n_lat, 0), (n_ctx, n_lat)) if need_ctx else ((n_lat, 0),):
        fwd, inv = dfts[n]
        y = _conv_fwd(fwd, u, v_col, spectra[n], 0, n, row0 // n)
        z1 = _conv_inv(inv, y, (u, row0, x1_col), (u, row0, v_col), bias, 0, n)
        y = _conv_fwd(fwd, z1, 0, spectra[n], 1, n, 0)
        outs.append(_conv_inv(inv, y, (u, row0, x2_col), (z1, 0, 0), bias, 1, n))
    return jnp.concatenate(outs, axis=1) if need_ctx else outs[0]


def _merge_kernel(x_ref, att_ref, hg_ref, hy_ref, ga_ref, gh_ref, gy_ref, gl_ref, gc_ref, wb_ref, wo_ref,
                  o_ref, *, n_lat, tm):
    def branch(gate_ref, val_ref, k):
        gate = jax.nn.sigmoid(gate_ref[0].astype(F32))
        return gate * jnp.dot(val_ref[0], wb_ref[k], preferred_element_type=F32)

    merged = branch(ga_ref, att_ref, 0) + branch(gh_ref, hg_ref, 1) + branch(gy_ref, hy_ref, 2)
    r = jnp.dot(merged.astype(BF16), wo_ref[...], preferred_element_type=F32)
    row = pl.program_id(1) * tm + lax.broadcasted_iota(jnp.int32, (tm, 1), 0)
    gate = jnp.where(row >= n_lat, gc_ref[0, 0], gl_ref[0, 0])
    o_ref[0] = x_ref[0] + gate * r


def _merge(x, att, hg, hy, z, mod, layer, ctx_row, wb_bf, wo_bf, n_lat, rows):
    bsz, _, d = x.shape
    seq = rows
    tm = _pick(seq, (512, 384, 256))
    tok = lambda: pl.BlockSpec((1, tm, d), lambda b, t: (b, t, 0))
    gate = lambda k: pl.BlockSpec((1, tm, d), lambda b, t: (b, t, COL_GATES * LANES // d + k))
    return pl.pallas_call(
        functools.partial(_merge_kernel, n_lat=n_lat, tm=tm),
        grid=(bsz, seq // tm),
        in_specs=[tok(), tok(), tok(), tok(), gate(0), gate(1), gate(2),
                  pl.BlockSpec((1, 1, 1, d), lambda b, t: (layer, b, 0, 2)),
                  pl.BlockSpec((1, 1, 1, d), lambda b, t: (layer, ctx_row, 0, 2)),
                  pl.BlockSpec((3, d, d), lambda b, t: (0, 0, 0)),
                  pl.BlockSpec((d, d), lambda b, t: (0, 0))],
        out_specs=tok(),
        out_shape=jax.ShapeDtypeStruct((bsz, seq, d), F32),
        compiler_params=pltpu.CompilerParams(dimension_semantics=("parallel", "parallel")),
        name="merge",
    )(x, att, hg, hy, z, z, z, mod, mod, wb_bf, wo_bf)


def _mlp_kernel(x_ref, shl_ref, scl_ref, shc_ref, scc_ref, gl_ref, gc_ref, g_ref, w1_ref, w2_ref, o_ref,
                h_ref, acc_ref, *, n_lat, tm):
    t = pl.program_id(1)
    j = pl.program_id(2)

    @pl.when(j == 0)
    def _():
        h = _modulated_norm(x_ref[0], g_ref[...], shl_ref[0, 0], scl_ref[0, 0], shc_ref[0, 0], scc_ref[0, 0],
                            t * tm, n_lat)
        h_ref[...] = h.astype(BF16)
        acc_ref[...] = jnp.zeros_like(acc_ref)

    a = jnp.maximum(jnp.dot(h_ref[...], w1_ref[...], preferred_element_type=F32), 0.0)
    acc_ref[...] += jnp.dot((a * a).astype(BF16), w2_ref[...], preferred_element_type=F32)

    @pl.when(j == pl.num_programs(2) - 1)
    def _():
        row = t * tm + lax.broadcasted_iota(jnp.int32, (tm, 1), 0)
        gate = jnp.where(row >= n_lat, gc_ref[0, 0], gl_ref[0, 0])
        o_ref[0] = x_ref[0] + gate * acc_ref[...]


def _mlp(x, mod, layer, ctx_row, g, w1_bf, w2_bf, n_lat):
    bsz, seq, d = x.shape
    ff = w1_bf.shape[1]
    tm = _pick(seq, (768, 512, 256))
    tf = _pick(ff, (1024, 512, 128))
    return pl.pallas_call(
        functools.partial(_mlp_kernel, n_lat=n_lat, tm=tm),
        grid=(bsz, seq // tm, ff // tf),
        in_specs=[pl.BlockSpec((1, tm, d), lambda b, t, j: (b, t, 0))]
        + _mod_specs(layer, ctx_row, 3, 4, d, 3)
        + [pl.BlockSpec((1, 1, 1, d), lambda b, t, j: (layer, b, 0, 5)),
           pl.BlockSpec((1, 1, 1, d), lambda b, t, j: (layer, ctx_row, 0, 5)),
           pl.BlockSpec((1, d), lambda b, t, j: (0, 0)),
           pl.BlockSpec((d, tf), lambda b, t, j: (0, j)),
           pl.BlockSpec((tf, d), lambda b, t, j: (j, 0))],
        out_specs=pl.BlockSpec((1, tm, d), lambda b, t, j: (b, t, 0)),
        out_shape=jax.ShapeDtypeStruct((bsz, seq, d), F32),
        scratch_shapes=[pltpu.VMEM((tm, d), BF16), pltpu.VMEM((tm, d), F32)],
        compiler_params=pltpu.CompilerParams(dimension_semantics=("parallel", "parallel", "arbitrary")),
        name="mlp",
    )(x, mod, mod, mod, mod, mod, mod, g.reshape(1, d), w1_bf, w2_bf)


def kernel(x, c, ctx, c_ctx, ada_w, ada_b, norm1_g, norm2_g, w_in, q_norm_g, k_norm_g, hg_lb_raw, hg_norm_g,
           hy_short_w, hy_short_b, hy_filt_w1, hy_filt_b1, hy_filt_w2, hy_filt_b2, hy_filt_w3, hy_freq, hy_bias,
           w_branch, w_out, w_mlp1, w_mlp2):
    bsz, n_lat, d = x.shape
    n_ctx = ctx.shape[1]
    depth = ada_w.shape[0]
    gate_cols = 3 * d

    cvec = jnp.concatenate([c, c_ctx[None, :]], axis=0)
    ctx_row = bsz
    pad_rows = -cvec.shape[0] % 8
    cvec = jnp.pad(cvec, ((0, pad_rows), (0, 0)))
    mod = _adaln(cvec, ada_w, ada_b)
    mod = mod.reshape(depth, cvec.shape[0], 1, 6 * d)

    cos, sin_signed = _rope_tables(n_lat, n_ctx)
    dfts = {n: _dft_matrices(n) for n in (n_lat, n_ctx)}

    xs = jnp.concatenate([x, ctx], axis=1)
    for l in range(depth):
        need_ctx = l < depth - 1
        w_l = w_in[l]
        w_bf = jnp.concatenate([w_l[:, -gate_cols:], w_l[:, :-gate_cols]], axis=1).astype(BF16)
        z = _inproj(xs, mod, l, ctx_row, norm1_g[l], w_bf, n_lat)

        att = _attention(z, cos, sin_signed, q_norm_g[l], k_norm_g[l], n_lat, need_ctx)
        hg = _hgrn(z, hg_lb_raw, hg_norm_g[l], l, n_lat, n_ctx)
        w1p = jnp.pad(hy_filt_w1[l], ((0, LANES - HY_EMB), (0, 0)))
        spectra = {n: _hyena_filters(n, w1p, hy_filt_b1[l], hy_filt_w2[l], hy_filt_b2[l], hy_filt_w3[l],
                                     hy_freq[l]) for n in ((n_lat, n_ctx) if need_ctx else (n_lat,))}
        hy = _hyena(z, hy_short_w[l], hy_short_b[l], spectra, dfts, hy_bias[l], n_lat, n_ctx, need_ctx)
        rows = n_lat + n_ctx if need_ctx else n_lat
        xs = _merge(xs, att, hg, hy, z, mod, l, ctx_row, w_branch[l].astype(BF16), w_out[l].astype(BF16), n_lat,
                    rows)
        xs = _mlp(xs, mod, l, ctx_row, norm2_g[l], w_mlp1[l].astype(BF16), w_mlp2[l].astype(BF16), n_lat)
    return xs
```

```python
import functools
import math

import jax
import jax.numpy as jnp
import numpy as np
from jax import lax
from jax.experimental import pallas as pl
from jax.experimental.pallas import tpu as pltpu

F32 = jnp.float32
BF16 = jnp.bfloat16

LANES = 128
HEAD_DIM = 128
ATT_HEADS = 8
ATT_KV_HEADS = 2
ATT_GROUP = ATT_HEADS // ATT_KV_HEADS
ROPE_THETA = 10000.0
GRID_W = 64
HG_HEADS = 8
HG_CHUNK = 64
HG_MIN_F = 1e-6
HG_HEADS_PER_STEP = 4
HY_BANDS = 16
HY_EMB = 1 + 2 * HY_BANDS
HY_FAST_DECAY = 0.3
HY_SLOW_DECAY = 1.5
HY_TARGET = 1e-2
HY_MIN_DECAY = math.log(HY_TARGET) / HY_SLOW_DECAY
HY_MAX_DECAY = math.log(HY_TARGET) / HY_FAST_DECAY
EPS = 1e-6
DFT_P = 256
HY_FWD_ROWS = 1024
HY_INV_ROWS = 512
HY_TILE = 512
HY_LANE_GROUP = 256
HIGHEST = lax.Precision.HIGHEST

COL_GATES = 0
COL_ATT_Q = 24
COL_ATT_K = 32
COL_ATT_V = 34
COL_HG_Q = 36
COL_HG_F = 44
COL_HG_B = 52
COL_HG_I = 60
COL_HG_G = 68
COL_HY = 76
N_COLS = 100


def _pick(n, candidates):
    for c in candidates:
        if n % c == 0:
            return c
    raise ValueError(f"no tile in {candidates} divides {n}")


def _nt_dot(a, b):
    return lax.dot_general(a, b, (((1,), (1,)), ((), ())), preferred_element_type=F32)


def _adaln_kernel(c_ref, w_ref, b_ref, o_ref):
    cv = c_ref[...]
    s = cv * jax.nn.sigmoid(cv)
    o_ref[0] = jnp.dot(s, w_ref[0], precision=HIGHEST, preferred_element_type=F32) + b_ref[0]


def _adaln(cvec, ada_w, ada_b):
    depth, d, n6 = ada_w.shape
    rows = cvec.shape[0]
    tn = _pick(n6, (1536, 1024, 512, 128))
    return pl.pallas_call(
        _adaln_kernel,
        grid=(depth, n6 // tn),
        in_specs=[pl.BlockSpec((rows, d), lambda l, j: (0, 0)),
                  pl.BlockSpec((1, d, tn), lambda l, j: (l, 0, j)),
                  pl.BlockSpec((1, 1, tn), lambda l, j: (l, 0, j))],
        out_specs=pl.BlockSpec((1, rows, tn), lambda l, j: (l, 0, j)),
        out_shape=jax.ShapeDtypeStruct((depth, rows, n6), F32),
        name="adaln",
    )(cvec, ada_w, ada_b.reshape(depth, 1, n6))


def _modulated_norm(x, g, sh_l, sc_l, sh_c, sc_c, row0, n_lat):
    y = x * lax.rsqrt(jnp.mean(x * x, axis=-1, keepdims=True) + EPS) * g
    row = row0 + lax.broadcasted_iota(jnp.int32, (x.shape[0], 1), 0)
    is_ctx = row >= n_lat
    sc = jnp.where(is_ctx, sc_c, sc_l)
    sh = jnp.where(is_ctx, sh_c, sh_l)
    return y * (1.0 + sc) + sh


def _mod_specs(layer, ctx_row, k_shift, k_scale, d, nargs):
    def spec(row_fn, k):
        if nargs == 3:
            return pl.BlockSpec((1, 1, 1, d), lambda b, t, j: (layer, row_fn(b), 0, k))
        return pl.BlockSpec((1, 1, 1, d), lambda b, t: (layer, row_fn(b), 0, k))
    lat = lambda b: b
    ctx = lambda b: ctx_row
    return [spec(lat, k_shift), spec(lat, k_scale), spec(ctx, k_shift), spec(ctx, k_scale)]


def _inproj_kernel(x_ref, shl_ref, scl_ref, shc_ref, scc_ref, g_ref, w_ref, o_ref, h_ref, *, n_lat, tm):
    t = pl.program_id(1)

    @pl.when(pl.program_id(2) == 0)
    def _():
        h = _modulated_norm(x_ref[0], g_ref[...], shl_ref[0, 0], scl_ref[0, 0], shc_ref[0, 0], scc_ref[0, 0],
                            t * tm, n_lat)
        h_ref[...] = h.astype(BF16)

    o_ref[0] = jnp.dot(h_ref[...], w_ref[...], preferred_element_type=F32).astype(BF16)


def _inproj(x, mod, layer, ctx_row, g, w_bf, n_lat):
    bsz, seq, d = x.shape
    n = w_bf.shape[1]
    tm = _pick(seq, (768, 512, 256))
    tn = _pick(n, (2560, 1280, 640, 128))
    return pl.pallas_call(
        functools.partial(_inproj_kernel, n_lat=n_lat, tm=tm),
        grid=(bsz, seq // tm, n // tn),
        in_specs=[pl.BlockSpec((1, tm, d), lambda b, t, j: (b, t, 0))]
        + _mod_specs(layer, ctx_row, 0, 1, d, 3)
        + [pl.BlockSpec((1, d), lambda b, t, j: (0, 0)),
           pl.BlockSpec((d, tn), lambda b, t, j: (0, j))],
        out_specs=pl.BlockSpec((1, tm, tn), lambda b, t, j: (b, t, j)),
        out_shape=jax.ShapeDtypeStruct((bsz, seq, n), BF16),
        scratch_shapes=[pltpu.VMEM((tm, d), BF16)],
        compiler_params=pltpu.CompilerParams(dimension_semantics=("parallel", "parallel", "arbitrary")),
        name="inproj",
    )(x, mod, mod, mod, mod, g.reshape(1, d), w_bf)


def _rope(x, cos, sin_signed):
    lane = lax.broadcasted_iota(jnp.int32, x.shape, 1)
    first = (lane % 64) < 32
    rx = jnp.where(first, pltpu.roll(x, 96, 1), pltpu.roll(x, 32, 1))
    return x * cos + rx * sin_signed


def _head_norm(x, g):
    return x * lax.rsqrt(jnp.mean(x * x, axis=-1, keepdims=True) + EPS) * g


def _attn_kernel(q_ref, k_ref, v_ref, cq_ref, sq_ref, ck_ref, sk_ref, gq_ref, gk_ref, o_ref, ks_ref, vs_ref,
                 *, n_lat, tq):
    qi = pl.program_id(2)

    @pl.when(qi == 0)
    def _():
        k = _head_norm(k_ref[0].astype(F32), gk_ref[...])
        ks_ref[...] = _rope(k, ck_ref[...], sk_ref[...]).astype(BF16)
        vs_ref[:, :HEAD_DIM] = v_ref[0]
        vs_ref[:, HEAD_DIM:] = jnp.ones((vs_ref.shape[0], HEAD_DIM), BF16)

    scale = HEAD_DIM ** -0.5 * math.log2(math.e)

    def attend(keys, vals):
        qall = q_ref[0].astype(F32)
        qs = []
        for g in range(ATT_GROUP):
            qg = _head_norm(qall[:, g * HEAD_DIM:(g + 1) * HEAD_DIM], gq_ref[...])
            qs.append((_rope(qg, cq_ref[...], sq_ref[...]) * scale).astype(BF16))
        ss = [_nt_dot(q, keys) for q in qs]
        ps = [jnp.exp2(s - jnp.max(s, axis=-1, keepdims=True)).astype(BF16) for s in ss]
        for g, p in enumerate(ps):
            o2 = jnp.dot(p, vals, preferred_element_type=F32)
            o = o2[:, :HEAD_DIM] / o2[:, HEAD_DIM:]
            o_ref[0, :, g * HEAD_DIM:(g + 1) * HEAD_DIM] = o.astype(BF16)

    is_lat = qi * tq < n_lat

    @pl.when(is_lat)
    def _():
        attend(ks_ref[...], vs_ref[...])

    @pl.when(jnp.logical_not(is_lat))
    def _():
        attend(ks_ref[n_lat:, :], vs_ref[n_lat:, :])


def _attention(z, cos, sin_signed, gq, gk, n_lat, need_ctx):
    bsz, seq, _ = z.shape
    tq = 256
    n_q = (seq if need_ctx else n_lat) // tq
    gw = ATT_GROUP * HEAD_DIM
    return pl.pallas_call(
        functools.partial(_attn_kernel, n_lat=n_lat, tq=tq),
        grid=(bsz, ATT_KV_HEADS, n_q),
        in_specs=[pl.BlockSpec((1, tq, gw), lambda b, h, i: (b, i, COL_ATT_Q // ATT_GROUP + h)),
                  pl.BlockSpec((1, seq, HEAD_DIM), lambda b, h, i: (b, 0, COL_ATT_K + h)),
                  pl.BlockSpec((1, seq, HEAD_DIM), lambda b, h, i: (b, 0, COL_ATT_V + h)),
                  pl.BlockSpec((tq, HEAD_DIM), lambda b, h, i: (i, 0)),
                  pl.BlockSpec((tq, HEAD_DIM), lambda b, h, i: (i, 0)),
                  pl.BlockSpec((seq, HEAD_DIM), lambda b, h, i: (0, 0)),
                  pl.BlockSpec((seq, HEAD_DIM), lambda b, h, i: (0, 0)),
                  pl.BlockSpec((1, HEAD_DIM), lambda b, h, i: (0, 0)),
                  pl.BlockSpec((1, HEAD_DIM), lambda b, h, i: (0, 0))],
        out_specs=pl.BlockSpec((1, tq, gw), lambda b, h, i: (b, i, h)),
        out_shape=jax.ShapeDtypeStruct((bsz, n_q * tq, ATT_HEADS * HEAD_DIM), BF16),
        scratch_shapes=[pltpu.VMEM((seq, HEAD_DIM), BF16), pltpu.VMEM((seq, 2 * HEAD_DIM), BF16)],
        compiler_params=pltpu.CompilerParams(dimension_semantics=("parallel", "parallel", "arbitrary")),
        name="attention",
    )(z, z, z, cos, sin_signed, cos, sin_signed, gq.reshape(1, HEAD_DIM), gk.reshape(1, HEAD_DIM))


def _rope_tables(n_lat, n_ctx):
    half = HEAD_DIM // 2
    inv = ROPE_THETA ** (-np.arange(0, half, 2, dtype=np.float64) / half)
    t = np.arange(n_lat)
    ar = (t // GRID_W)[:, None] * inv
    ac = (t % GRID_W)[:, None] * inv
    ang = np.concatenate([ar, ar, ac, ac], axis=-1)
    cos = np.concatenate([np.cos(ang), np.ones((n_ctx, HEAD_DIM))], axis=0)
    sin = np.concatenate([np.sin(ang), np.zeros((n_ctx, HEAD_DIM))], axis=0)
    sign = np.where((np.arange(HEAD_DIM) % 64) < 32, -1.0, 1.0)
    return jnp.asarray(cos, F32), jnp.asarray(sin * sign, F32)


def _hgrn_masks():
    c = HG_CHUNK
    idx = np.arange(c)
    t, s = idx[:, None], idx[None, :]

    def halving(size):
        half = size // 2
        return (t // size == s // size) & (t % size >= half) & (s % size < half)

    diag = (t // 8 == s // 8) & (s <= t)
    flip = lambda m: m[::-1, ::-1]
    fwd = np.stack([np.concatenate([halving(64), halving(32)], axis=1),
                    np.concatenate([halving(16), diag], axis=1)])
    bwd = np.stack([np.concatenate([flip(halving(64)), flip(halving(32))], axis=1),
                    np.concatenate([flip(halving(16)), flip(diag)], axis=1)])
    return jnp.asarray(np.stack([fwd, bwd]), F32)


def _hgrn_exponents(zg, lb, reverse):
    tile = 8
    nt = HG_CHUNK // tile
    sig = jax.nn.sigmoid(zg.astype(F32))
    f = lb + (1.0 - lb) * sig
    lf = jnp.log2(jnp.maximum(f, HG_MIN_F))
    kk = 1.0 - f
    p = lf.reshape(nt, tile, LANES)
    sub = lax.broadcasted_iota(jnp.int32, p.shape, 1)
    for k in (1, 2, 4):
        if reverse:
            p = p + jnp.where(sub < tile - k, pltpu.roll(p, tile - k, 1), 0.0)
        else:
            p = p + jnp.where(sub >= k, pltpu.roll(p, k, 1), 0.0)
    order = list(range(nt - 1, -1, -1)) if reverse else list(range(nt))
    tot_row, mid_row = (0, 4) if reverse else (tile - 1, 3)
    s = [p[j] for j in order]
    tb = [jnp.broadcast_to(p[j, tot_row:tot_row + 1, :], (tile, LANES)) for j in order]
    mb = [jnp.broadcast_to(p[j, mid_row:mid_row + 1, :], (tile, LANES)) for j in order]
    t01, t23, t45 = tb[0] + tb[1], tb[2] + tb[3], tb[4] + tb[5]
    t123, t456 = tb[1] + t23, t45 + tb[6]
    c4 = t01 + t23
    before = [None, tb[0], t01, t01 + tb[2], c4, c4 + tb[4], c4 + t45, c4 + t456]
    total = before[7] + tb[7]
    cum = [s[0]] + [before[k] + s[k] for k in range(1, nt)]
    suf = [total - cum[k] for k in range(nt)]
    w8 = [s[k] - mb[k] for k in range(nt)]
    w16 = [s[k] if k % 2 else tb[k] - s[k] for k in range(nt)]
    w32 = [t01 - s[0], tb[1] - s[1], s[2], tb[2] + s[3], t45 - s[4], tb[5] - s[5], s[6], tb[6] + s[7]]
    w64 = [c4 - s[0], t123 - s[1], t23 - s[2], tb[3] - s[3], s[4], tb[4] + s[5], t45 + s[6], t456 + s[7]]

    def natural(parts):
        return jnp.concatenate([parts[order.index(j)] for j in range(nt)], axis=0)

    ex = {"cum": natural(cum), "suf": natural(suf), 64: natural(w64), 32: natural(w32), 16: natural(w16),
          8: natural(w8), "tot": total[0:1]}
    return kk, ex


def _hgrn_scores(q, kk, ex, st):
    qf = q.astype(F32)
    inter = _nt_dot((qf * jnp.exp2(ex["cum"])).astype(BF16), st.astype(BF16))
    zeros = jnp.zeros(q.shape, BF16)
    pairs = []
    for la, lb_ in ((64, 32), (16, 8)):
        ea, eb = jnp.exp2(ex[la]), jnp.exp2(ex[lb_])
        eb_k = jnp.exp2(-ex[lb_]) if lb_ == 8 else eb
        qa = jnp.concatenate([(qf * ea).astype(BF16), (qf * eb).astype(BF16)], axis=1)
        ka = jnp.concatenate([(kk * ea).astype(BF16), zeros], axis=1)
        kb = jnp.concatenate([zeros, (kk * eb_k).astype(BF16)], axis=1)
        pairs.append(_nt_dot(qa, jnp.concatenate([ka, kb], axis=0)))
    return inter, pairs


def _hgrn_output(inter, pairs, v, mask):
    p0 = pairs[0] * mask[0]
    p1 = jnp.where(mask[1] > 0.0, pairs[1], 0.0)
    probs = jnp.concatenate([p0.astype(BF16), p1.astype(BF16)], axis=1)
    return inter + jnp.dot(probs, jnp.concatenate([v, v, v, v], axis=0), preferred_element_type=F32)


def _hgrn_state(kk, ex, v, st):
    k_suf = (kk * jnp.exp2(ex["suf"])).astype(BF16)
    upd = lax.dot_general(v, k_suf, (((0,), (0,)), ((), ())), preferred_element_type=F32)
    return st * jnp.exp2(ex["tot"]) + upd


def _hgrn_kernel(q_ref, zf_ref, zb_ref, i_ref, g_ref, lbraw_ref, gn_ref, mask_ref, o_ref,
                 acc_ref, st_ref, *, layer, n_lat, n_ctx):
    c = HG_CHUNK
    nc_lat, nc_ctx = n_lat // c, n_ctx // c
    nc = nc_lat + nc_ctx
    raw = lbraw_ref[...].astype(F32)
    ew = jnp.exp(raw - jnp.max(raw, axis=0, keepdims=True))
    sm = ew / jnp.sum(ew, axis=0, keepdims=True)
    lower = jnp.zeros_like(sm[0])
    for dpt in range(1, layer + 1):
        lower = lower + sm[dpt]

    acc_ref[...] = jnp.zeros_like(acc_ref)
    st_ref[...] = jnp.zeros_like(st_ref)

    def step(n, carry):
        cf = jnp.where(n < nc_ctx, nc_lat + n, n - nc_ctx)
        cb = nc - 1 - n
        chains = []
        for hh in range(HG_HEADS_PER_STEP):
            lanes = slice(hh * LANES, (hh + 1) * LANES)
            for d, (chunk, z_ref) in enumerate(((cf, zf_ref), (cb, zb_ref))):
                rows = pl.ds(pl.multiple_of(chunk * c, c), c)
                chains.append((hh, d, rows, lanes, z_ref))
        gates = [_hgrn_exponents(z_ref[0, rows, lanes], lower[hh, d:d + 1], d == 1)
                 for hh, d, rows, lanes, z_ref in chains]
        scored = [_hgrn_scores(q_ref[0, rows, lanes], kk, ex, st_ref[hh, d])
                  for (hh, d, rows, lanes, _), (kk, ex) in zip(chains, gates)]
        for (hh, d, rows, lanes, _), (inter, pairs) in zip(chains, scored):
            out = _hgrn_output(inter, pairs, i_ref[0, rows, lanes], mask_ref[d])
            acc_ref[rows, lanes] = acc_ref[rows, lanes] + out
        for (hh, d, rows, lanes, _), (kk, ex) in zip(chains, gates):
            st_ref[hh, d] = _hgrn_state(kk, ex, i_ref[0, rows, lanes], st_ref[hh, d])
        return carry

    lax.fori_loop(0, nc, step, 0)

    for hh in range(HG_HEADS_PER_STEP):
        lanes = slice(hh * LANES, (hh + 1) * LANES)
        o = acc_ref[:, lanes]
        y = o * lax.rsqrt(jnp.mean(o * o, axis=-1, keepdims=True) + EPS) * gn_ref[...]
        g = g_ref[0, :, lanes].astype(F32)
        o_ref[0, :, lanes] = (y * (g * jax.nn.sigmoid(g))).astype(BF16)


def _hgrn(z, lb_raw, g_norm, layer, n_lat, n_ctx):
    bsz, seq, _ = z.shape
    depth = lb_raw.shape[0]
    hps = HG_HEADS_PER_STEP
    width = hps * LANES
    masks = _hgrn_masks()
    lbr = lb_raw.reshape(depth, 2, HG_HEADS, LANES).transpose(0, 2, 1, 3)
    col = lambda c0: pl.BlockSpec((1, seq, width), lambda b, h: (b, 0, c0 // hps + h))
    return pl.pallas_call(
        functools.partial(_hgrn_kernel, layer=layer, n_lat=n_lat, n_ctx=n_ctx),
        grid=(bsz, HG_HEADS // hps),
        in_specs=[col(COL_HG_Q), col(COL_HG_F), col(COL_HG_B), col(COL_HG_I), col(COL_HG_G),
                  pl.BlockSpec((depth, hps, 2, LANES), lambda b, h: (0, h, 0, 0)),
                  pl.BlockSpec((1, LANES), lambda b, h: (0, 0)),
                  pl.BlockSpec((2, 2, HG_CHUNK, LANES), lambda b, h: (0, 0, 0, 0))],
        out_specs=pl.BlockSpec((1, seq, width), lambda b, h: (b, 0, h)),
        out_shape=jax.ShapeDtypeStruct((bsz, seq, HG_HEADS * LANES), BF16),
        scratch_shapes=[pltpu.VMEM((seq, width), F32), pltpu.VMEM((hps, 2, LANES, LANES), F32)],
        compiler_params=pltpu.CompilerParams(dimension_semantics=("parallel", "parallel")),
        name="hgrn",
    )(z, z, z, z, z, lbr, g_norm.reshape(1, LANES), masks)


def _dft_matrices(n):
    size = 2 * n
    k = np.arange(n)[:, None]
    t = np.arange(n)[None, :]
    ang = (2.0 * np.pi / size) * ((k * t) % size)
    cos, sin = np.cos(ang), np.sin(ang)
    sin[0, :] = np.where(np.arange(n) % 2 == 0, 1.0, -1.0)
    fwd = np.concatenate([cos.reshape(n // DFT_P, DFT_P, n), sin.reshape(n // DFT_P, DFT_P, n)], axis=1)
    fwd = fwd.reshape(size, n)
    return jnp.asarray(fwd, BF16), jnp.asarray(fwd.T.copy(), BF16)


def _hyena_positions(n):
    t = np.linspace(0.0, 1.0, n)[:, None]
    w = (2.0 * math.pi / n) * np.arange(n)[:, None]
    f = np.linspace(1e-4, HY_BANDS - 1, HY_BANDS)[None, :]
    z = np.concatenate([t, np.cos(f * w), -np.sin(f * w)], axis=-1)
    return jnp.asarray(np.pad(z, ((0, 0), (0, LANES - HY_EMB))), F32)


def _filter_kernel(pos_ref, w1_ref, b1_ref, w2_ref, b2_ref, fr_ref, w3f_ref, w3b_ref, dl_ref, fs_ref, fd_ref,
                   h_ref):
    pos = pos_ref[...]

    @pl.when((pl.program_id(0) == 0) & (pl.program_id(1) == 0))
    def _():
        h1 = jnp.sin(fr_ref[0:1] * (jnp.dot(pos, w1_ref[...], precision=HIGHEST, preferred_element_type=F32)
                                    + b1_ref[...]))
        h_ref[...] = jnp.sin(fr_ref[1:2] * (jnp.dot(h1, w2_ref[...], precision=HIGHEST,
                                                    preferred_element_type=F32) + b2_ref[...]))

    h = h_ref[...]
    window = jnp.exp(-pos[:, 0:1] * dl_ref[...])
    hf = jnp.dot(h, w3f_ref[...], precision=HIGHEST, preferred_element_type=F32) * window
    hb = jnp.dot(h, w3b_ref[...], precision=HIGHEST, preferred_element_type=F32) * window
    row = lax.broadcasted_iota(jnp.int32, (pos.shape[0], 1), 0)
    hb = jnp.where(row == 0, 0.0, hb)
    norm = jnp.sum(jnp.abs(hf), axis=0, keepdims=True) + jnp.sum(jnp.abs(hb), axis=0, keepdims=True) + EPS
    inv = 1.0 / norm
    fs_ref[0] = ((hf + hb) * inv).astype(BF16)
    fd_ref[0] = ((hf - hb) * inv).astype(BF16)


def _filter_dft_kernel(a_ref, fs_ref, fd_ref, o_ref, *, n):
    m = pl.program_id(2)
    p = DFT_P
    hc = jnp.dot(a_ref[:p], fs_ref[0], preferred_element_type=F32)
    hs = jnp.dot(a_ref[p:], fd_ref[0], preferred_element_type=F32)
    k = m * p + lax.broadcasted_iota(jnp.int32, (p, 1), 0)
    weight = jnp.where(k == 0, 1.0, 2.0) / (2 * n)
    o_ref[0, :p] = hc * weight
    o_ref[0, p:] = hs * weight

    @pl.when(m == 0)
    def _():
        hny = jnp.dot(a_ref[p:p + 16], fs_ref[0], preferred_element_type=F32)[:8]
        first = lax.broadcasted_iota(jnp.int32, (8, 1), 0) == 0
        o_ref[0, p:p + 8] = jnp.where(first, hny / (2 * n), hs[:8] * weight[:8])


def _hyena_filters(n, fwd, w1p, b1, w2, b2, w3, freq):
    hid = w2.shape[0]
    c = w3.shape[1] // 4
    tn = _pick(c, (512, 256, 128))
    nct = c // tn
    pos = _hyena_positions(n)
    deltas = jnp.asarray(np.abs(np.linspace(HY_MIN_DECAY, HY_MAX_DECAY, c))[None, :], F32)
    full = lambda shape: pl.BlockSpec(shape, lambda o, j: (0,) * len(shape))
    fsum, fdiff = pl.pallas_call(
        _filter_kernel,
        grid=(2, nct),
        in_specs=[full((n, LANES)), full((LANES, hid)), full((1, hid)), full((hid, hid)), full((1, hid)),
                  full((2, hid)),
                  pl.BlockSpec((hid, tn), lambda o, j: (0, (2 * o) * nct + j)),
                  pl.BlockSpec((hid, tn), lambda o, j: (0, (2 * o + 1) * nct + j)),
                  pl.BlockSpec((1, tn), lambda o, j: (0, j))],
        out_specs=[pl.BlockSpec((1, n, tn), lambda o, j: (o, 0, j))] * 2,
        out_shape=[jax.ShapeDtypeStruct((2, n, c), BF16)] * 2,
        scratch_shapes=[pltpu.VMEM((n, hid), F32)],
        compiler_params=pltpu.CompilerParams(dimension_semantics=("arbitrary", "arbitrary")),
        name="hyena_filter",
    )(pos, w1p, b1.reshape(1, hid), w2, b2.reshape(1, hid), freq, w3, w3, deltas)
    return pl.pallas_call(
        functools.partial(_filter_dft_kernel, n=n),
        grid=(2, nct, n // DFT_P),
        in_specs=[pl.BlockSpec((2 * DFT_P, n), lambda o, j, m: (m, 0)),
                  pl.BlockSpec((1, n, tn), lambda o, j, m: (o, 0, j)),
                  pl.BlockSpec((1, n, tn), lambda o, j, m: (o, 0, j))],
        out_specs=pl.BlockSpec((1, 2 * DFT_P, tn), lambda o, j, m: (o, m, j)),
        out_shape=jax.ShapeDtypeStruct((2, 2 * n, c), F32),
        name="hyena_filter_dft",
    )(fwd, fsum, fdiff)


def _short_conv(u, w, b):
    n = u.shape[0]
    row = lax.broadcasted_iota(jnp.int32, (n, 1), 0)
    prev = jnp.where(row == 0, 0.0, pltpu.roll(u, 1, 0))
    nxt = jnp.where(row == n - 1, 0.0, pltpu.roll(u, n - 1, 0))
    return prev * w[0:1] + u * w[1:2] + nxt * w[2:3] + b


def _shortconv_kernel(u_ref, w_ref, b_ref, o_ref):
    o_ref[0] = _short_conv(u_ref[0].astype(F32), w_ref[...], b_ref[...]).astype(BF16)


def _shortconv(z, w, b, n, row_blk):
    bsz = z.shape[0]
    width = w.shape[1]
    tn = 256
    c0 = COL_HY * LANES // tn
    return pl.pallas_call(
        _shortconv_kernel,
        grid=(bsz, width // tn),
        in_specs=[pl.BlockSpec((1, n, tn), lambda bb, j: (bb, row_blk, c0 + j)),
                  pl.BlockSpec((3, tn), lambda bb, j: (0, j)),
                  pl.BlockSpec((1, tn), lambda bb, j: (0, j))],
        out_specs=pl.BlockSpec((1, n, tn), lambda bb, j: (bb, 0, j)),
        out_shape=jax.ShapeDtypeStruct((bsz, n, width), BF16),
        compiler_params=pltpu.CompilerParams(dimension_semantics=("parallel", "parallel")),
        name="hyena_shortconv",
    )(z, w, b.reshape(1, width))


def _conv_fwd_kernel(a_ref, v_ref, h_ref, y_ref, *, chunks):
    p = DFT_P
    zf = jnp.dot(a_ref[...], v_ref[0], preferred_element_type=F32)
    for k in range(chunks):
        lo = 2 * p * k
        zc, zs = zf[lo:lo + p], zf[lo + p:lo + 2 * p]
        hc, hs = h_ref[0, lo:lo + p], h_ref[0, lo + p:lo + 2 * p]
        yc, ys = zc * hc - zs * hs, zc * hs + zs * hc
        if k == 0:
            real_row = (lax.broadcasted_iota(jnp.int32, (p, 1), 0) == 0) & (pl.program_id(1) == 0)
            yc = jnp.where(real_row, zc * hc, yc)
            ys = jnp.where(real_row, zs * hs, ys)
        y_ref[0, lo:lo + p] = yc.astype(BF16)
        y_ref[0, lo + p:lo + 2 * p] = ys.astype(BF16)


def _conv_fwd(fwd, v_arr, v_col, spec, order, n, row_blk):
    bsz = v_arr.shape[0]
    c = spec.shape[2]
    rows = min(HY_FWD_ROWS, 2 * n)
    return pl.pallas_call(
        functools.partial(_conv_fwd_kernel, chunks=rows // (2 * DFT_P)),
        grid=(bsz, 2 * n // rows),
        in_specs=[pl.BlockSpec((rows, n), lambda b, m: (m, 0)),
                  pl.BlockSpec((1, n, c), lambda b, m: (b, row_blk, v_col)),
                  pl.BlockSpec((1, rows, c), lambda b, m: (order, m, 0))],
        out_specs=pl.BlockSpec((1, rows, c), lambda b, m: (b, m, 0)),
        out_shape=jax.ShapeDtypeStruct((bsz, 2 * n, c), BF16),
        compiler_params=pltpu.CompilerParams(dimension_semantics=("parallel", "arbitrary")),
        name="hyena_conv_fwd",
    )(fwd, v_arr, spec)


def _conv_inv_kernel(g_ref, y_ref, gate_ref, zin_ref, bias_ref, o_ref):
    y = jnp.dot(g_ref[...], y_ref[0], preferred_element_type=F32)
    o_ref[0] = (gate_ref[0].astype(F32) * (y + zin_ref[0].astype(F32) * bias_ref[0])).astype(BF16)


def _conv_inv(inv, y, gate, zin, bias, order, n):
    bsz, _, c = y.shape
    tm = min(HY_INV_ROWS, n)

    def rows_of(triple):
        _, row0, col = triple
        return pl.BlockSpec((1, tm, c), lambda b, m: (b, row0 // tm + m, col))

    return pl.pallas_call(
        _conv_inv_kernel,
        grid=(bsz, n // tm),
        in_specs=[pl.BlockSpec((tm, 2 * n), lambda b, m: (m, 0)),
                  pl.BlockSpec((1, 2 * n, c), lambda b, m: (b, 0, 0)),
                  rows_of(gate), rows_of(zin),
                  pl.BlockSpec((1, 1, c), lambda b, m: (order, 0, 0))],
        out_specs=pl.BlockSpec((1, tm, c), lambda b, m: (b, m, 0)),
        out_shape=jax.ShapeDtypeStruct((bsz, n, c), BF16),
        compiler_params=pltpu.CompilerParams(dimension_semantics=("parallel", "arbitrary")),
        name="hyena_conv_inv",
    )(inv, y, gate[0], zin[0], bias.reshape(2, 1, c))


def _hyena_dense(z, sw, sb, spectrum, dft, bias, n, row_blk):
    u = _shortconv(z, sw, sb, n, row_blk)
    x1_col, x2_col, v_col = 0, 1, 2
    fwd, inv = dft
    y = _conv_fwd(fwd, u, v_col, spectrum, 0, n, 0)
    z1 = _conv_inv(inv, y, (u, 0, x1_col), (u, 0, v_col), bias, 0, n)
    y = _conv_fwd(fwd, z1, 0, spectrum, 1, n, 0)
    return _conv_inv(inv, y, (u, 0, x2_col), (z1, 0, 0), bias, 1, n)


def _radix8_frequencies(n):
    p = n // 8
    j = np.arange(p)
    return [8 * j, 4 + 8 * j, 1 + 8 * j, 1 + 8 * (j + p), 2 + 8 * j, 2 + 8 * (j + p), 3 + 8 * j, 3 + 8 * (j + p)]


def _radix8_matrices(n):
    size, p = 2 * n, n // 8
    freqs = _radix8_frequencies(n)

    def cos_sin(k, length):
        ang = (2.0 * np.pi / size) * ((k[:, None] * np.arange(length)[None, :]) % size)
        return np.cos(ang), np.sin(ang)

    def nyquist(sin_rows, length):
        sin_rows[0, :] = np.where(np.arange(length) % 2 == 0, 1.0, -1.0)
        return sin_rows

    filt = []
    for idx, k in enumerate(freqs):
        c, s = cos_sin(k, n)
        filt += [c, nyquist(s, n) if idx == 0 else s]
    slab = size // 8
    c0, s0 = cos_sin(freqs[0], slab)
    c4, s4 = cos_sin(freqs[1], slab)
    mats = [np.concatenate([c0, nyquist(s0, slab)]), np.concatenate([c4, s4])]
    for k1 in (1, 2, 3):
        ca, sa = cos_sin(freqs[2 * k1], slab)
        cb, sb = cos_sin(freqs[2 * k1 + 1], slab)
        mats.append(np.block([[ca, sa], [sa, -ca], [cb, sb], [sb, -cb]]))
    to_bf = lambda a: jnp.asarray(a, BF16)
    return to_bf(np.concatenate(filt)), [to_bf(m) for m in mats], [to_bf(m.T.copy()) for m in mats]


def _conv_fwd8_kernel(*refs, pre_conv):
    if pre_conv:
        v_ref, w_ref, b_ref, r0_ref, r4_ref, r1_ref, r2_ref, r3_ref, h_ref, y_ref, vs_ref = refs
    else:
        v_ref, r0_ref, r4_ref, r1_ref, r2_ref, r3_ref, h_ref, y_ref = refs
    n, tn = v_ref.shape[1], v_ref.shape[2]
    slab, p = n // 4, n // 8
    bf = lambda x: x.astype(BF16)
    cat = lambda re, im: jnp.concatenate([bf(re), bf(im)], axis=0)
    groups = [slice(g * HY_LANE_GROUP, (g + 1) * HY_LANE_GROUP) for g in range(tn // HY_LANE_GROUP)]
    inputs = []
    for lanes in groups:
        v = v_ref[0, :, lanes].astype(F32)
        if pre_conv:
            v = _short_conv(v, w_ref[:, lanes], b_ref[:, lanes])
            vs_ref[0, :, lanes] = v.astype(BF16)
        z0, z1, z2, z3 = (v[i * slab:(i + 1) * slab] for i in range(4))
        e, o = z0 + z2, z1 + z3
        a, b = (z1 - z3) * (0.5 ** 0.5), o * (0.5 ** 0.5)
        inputs.append((bf(e + o), bf(e - o), cat(z0 + a, -z2 - b), cat(z0 - z2, z3 - z1), cat(z0 - a, z2 - b)))
    spectra = [[jnp.dot(r[...], x, preferred_element_type=F32)
                for r, x in zip((r0_ref, r4_ref, r1_ref, r2_ref, r3_ref), xs)] for xs in inputs]
    for lanes, spec in zip(groups, spectra):
        chunks = [spec[0], spec[1]] + [s[i * 2 * p:(i + 1) * 2 * p] for s in spec[2:] for i in range(2)]
        for c, zf in enumerate(chunks):
            lo = 2 * p * c
            zc, zs = zf[:p], zf[p:]
            hc, hs = h_ref[0, lo:lo + p, lanes], h_ref[0, lo + p:lo + 2 * p, lanes]
            yc, ys = zc * hc - zs * hs, zc * hs + zs * hc
            if c == 0:
                real_row = lax.broadcasted_iota(jnp.int32, (p, 1), 0) == 0
                yc = jnp.where(real_row, zc * hc, yc)
                ys = jnp.where(real_row, zs * hs, ys)
            y_ref[0, lo:lo + p, lanes] = yc.astype(BF16)
            y_ref[0, lo + p:lo + 2 * p, lanes] = ys.astype(BF16)


def _const_spec(shape):
    return pl.BlockSpec(shape, lambda j, b: (0,) * len(shape), pipeline_mode=pl.Buffered(1))


def _conv_fwd8(mats, v_arr, v_col, spec, order, n, short=None):
    bsz = v_arr.shape[0]
    c = spec.shape[2]
    tn = HY_TILE
    in_specs = [pl.BlockSpec((1, n, tn), lambda j, b: (b, 0, v_col + j))]
    args = [v_arr]
    out_specs = [pl.BlockSpec((1, 2 * n, tn), lambda j, b: (b, 0, j))]
    out_shape = [jax.ShapeDtypeStruct((bsz, 2 * n, c), BF16)]
    if short is not None:
        w, bvec, wcol = short
        in_specs += [pl.BlockSpec((3, tn), lambda j, b: (0, wcol + j)),
                     pl.BlockSpec((1, tn), lambda j, b: (0, wcol + j))]
        args += [w, bvec.reshape(1, -1)]
        out_specs.append(pl.BlockSpec((1, n, tn), lambda j, b: (b, 0, j)))
        out_shape.append(jax.ShapeDtypeStruct((bsz, n, c), BF16))
    in_specs += [_const_spec(m.shape) for m in mats]
    in_specs.append(pl.BlockSpec((1, 2 * n, tn), lambda j, b: (order, 0, j)))
    res = pl.pallas_call(
        functools.partial(_conv_fwd8_kernel, pre_conv=short is not None),
        grid=(c // tn, bsz),
        in_specs=in_specs,
        out_specs=out_specs,
        out_shape=out_shape,
        compiler_params=pltpu.CompilerParams(dimension_semantics=("parallel", "parallel")),
        name="hyena_conv_fwd8",
    )(*args, *mats, spec)
    return res if short is not None else res[0]


def _conv_inv8_kernel(y_ref, r0_ref, r4_ref, r1_ref, r2_ref, r3_ref, gate_ref, w_ref, b_ref, zin_ref, bias_ref,
                      o_ref):
    n, tn = o_ref.shape[1], o_ref.shape[2]
    slab = n // 4
    groups = [slice(g * HY_LANE_GROUP, (g + 1) * HY_LANE_GROUP) for g in range(tn // HY_LANE_GROUP)]
    bounds = (0, slab, 2 * slab, 4 * slab, 6 * slab, 8 * slab)
    gates = [_short_conv(gate_ref[0, :, lanes].astype(F32), w_ref[:, lanes], b_ref[:, lanes]) for lanes in groups]
    skips = [zin_ref[0, :, lanes].astype(F32) * bias_ref[0, :, lanes] for lanes in groups]
    parts = [[jnp.dot(r[...], y_ref[0, lo:hi, lanes], preferred_element_type=F32)
              for r, lo, hi in zip((r0_ref, r4_ref, r1_ref, r2_ref, r3_ref), bounds[:-1], bounds[1:])]
             for lanes in groups]
    for lanes, gate, skip, (v0, v4, v1, v2, v3) in zip(groups, gates, skips, parts):
        (v1r, v1i), (v2r, v2i), (v3r, v3i) = ((v[:slab], v[slab:]) for v in (v1, v2, v3))
        s, d = v0 + v4, v0 - v4
        r = 0.5 ** 0.5
        outs = [s + v1r + v2r + v3r,
                d + (v1r - v1i - v3r - v3i) * r - v2i,
                s - v1i - v2r + v3i,
                d + (v3r - v3i - v1r - v1i) * r + v2i]
        for i, y in enumerate(outs):
            rows = slice(i * slab, (i + 1) * slab)
            o_ref[0, rows, lanes] = (gate[rows] * (y + skip[rows])).astype(BF16)


def _conv_inv8(mats_t, y, gate_arr, gate_col, short, zin, bias, order, n):
    bsz, _, c = y.shape
    tn = HY_TILE
    w, bvec, wcol = short
    return pl.pallas_call(
        _conv_inv8_kernel,
        grid=(c // tn, bsz),
        in_specs=[pl.BlockSpec((1, 2 * n, tn), lambda j, b: (b, 0, j))]
        + [_const_spec(m.shape) for m in mats_t]
        + [pl.BlockSpec((1, n, tn), lambda j, b: (b, 0, gate_col + j)),
           pl.BlockSpec((3, tn), lambda j, b: (0, wcol + j)),
           pl.BlockSpec((1, tn), lambda j, b: (0, wcol + j)),
           pl.BlockSpec((1, n, tn), lambda j, b: (b, 0, j)),
           pl.BlockSpec((1, 1, tn), lambda j, b: (order, 0, j))],
        out_specs=pl.BlockSpec((1, n, tn), lambda j, b: (b, 0, j)),
        out_shape=jax.ShapeDtypeStruct((bsz, n, c), BF16),
        compiler_params=pltpu.CompilerParams(dimension_semantics=("parallel", "parallel")),
        name="hyena_conv_inv8",
    )(y, *mats_t, gate_arr, w, bvec.reshape(1, -1), zin, bias.reshape(2, 1, c))


def _hyena_radix8(z, sw, sb, spectrum, mats, mats_t, bias, n):
    c = bias.shape[1]
    per = c // HY_TILE
    z_col = COL_HY * LANES // HY_TILE
    y, vs = _conv_fwd8(mats, z, z_col + 2 * per, spectrum, 0, n, short=(sw, sb, 2 * per))
    z1 = _conv_inv8(mats_t, y, z, z_col, (sw, sb, 0), vs, bias, 0, n)
    y = _conv_fwd8(mats, z1, 0, spectrum, 1, n)
    return _conv_inv8(mats_t, y, z, z_col + per, (sw, sb, per), z1, bias, 1, n)


def _merge_kernel(x_ref, att_ref, hg_ref, hy_ref, ga_ref, gh_ref, gy_ref, gl_ref, gc_ref, wb_ref, wo_ref,
                  o_ref, *, n_lat, tm):
    def branch(gate_ref, val_ref, k):
        gate = jax.nn.sigmoid(gate_ref[0].astype(F32))
        return gate * jnp.dot(val_ref[0], wb_ref[k], preferred_element_type=F32)

    merged = branch(ga_ref, att_ref, 0) + branch(gh_ref, hg_ref, 1) + branch(gy_ref, hy_ref, 2)
    r = jnp.dot(merged.astype(BF16), wo_ref[...], preferred_element_type=F32)
    row = pl.program_id(1) * tm + lax.broadcasted_iota(jnp.int32, (tm, 1), 0)
    gate = jnp.where(row >= n_lat, gc_ref[0, 0], gl_ref[0, 0])
    o_ref[0] = x_ref[0] + gate * r


def _merge(x, att, hg, hy, z, mod, layer, ctx_row, wb_bf, wo_bf, n_lat, rows):
    bsz, _, d = x.shape
    seq = rows
    tm = _pick(seq, (512, 384, 256))
    tok = lambda: pl.BlockSpec((1, tm, d), lambda b, t: (b, t, 0))
    gate = lambda k: pl.BlockSpec((1, tm, d), lambda b, t: (b, t, COL_GATES * LANES // d + k))
    return pl.pallas_call(
        functools.partial(_merge_kernel, n_lat=n_lat, tm=tm),
        grid=(bsz, seq // tm),
        in_specs=[tok(), tok(), tok(), tok(), gate(0), gate(1), gate(2),
                  pl.BlockSpec((1, 1, 1, d), lambda b, t: (layer, b, 0, 2)),
                  pl.BlockSpec((1, 1, 1, d), lambda b, t: (layer, ctx_row, 0, 2)),
                  pl.BlockSpec((3, d, d), lambda b, t: (0, 0, 0)),
                  pl.BlockSpec((d, d), lambda b, t: (0, 0))],
        out_specs=tok(),
        out_shape=jax.ShapeDtypeStruct((bsz, seq, d), F32),
        compiler_params=pltpu.CompilerParams(dimension_semantics=("parallel", "parallel")),
        name="merge",
    )(x, att, hg, hy, z, z, z, mod, mod, wb_bf, wo_bf)


def _mlp_kernel(x_ref, shl_ref, scl_ref, shc_ref, scc_ref, gl_ref, gc_ref, g_ref, w1_ref, w2_ref, o_ref,
                h_ref, acc_ref, *, n_lat, tm):
    t = pl.program_id(1)
    j = pl.program_id(2)

    @pl.when(j == 0)
    def _():
        h = _modulated_norm(x_ref[0], g_ref[...], shl_ref[0, 0], scl_ref[0, 0], shc_ref[0, 0], scc_ref[0, 0],
                            t * tm, n_lat)
        h_ref[...] = h.astype(BF16)
        acc_ref[...] = jnp.zeros_like(acc_ref)

    a = jnp.maximum(jnp.dot(h_ref[...], w1_ref[...], preferred_element_type=F32), 0.0)
    acc_ref[...] += jnp.dot((a * a).astype(BF16), w2_ref[...], preferred_element_type=F32)

    @pl.when(j == pl.num_programs(2) - 1)
    def _():
        row = t * tm + lax.broadcasted_iota(jnp.int32, (tm, 1), 0)
        gate = jnp.where(row >= n_lat, gc_ref[0, 0], gl_ref[0, 0])
        o_ref[0] = x_ref[0] + gate * acc_ref[...]


def _mlp(x, mod, layer, ctx_row, g, w1_bf, w2_bf, n_lat):
    bsz, seq, d = x.shape
    ff = w1_bf.shape[1]
    tm = _pick(seq, (768, 512, 256))
    tf = _pick(ff, (1024, 512, 128))
    return pl.pallas_call(
        functools.partial(_mlp_kernel, n_lat=n_lat, tm=tm),
        grid=(bsz, seq // tm, ff // tf),
        in_specs=[pl.BlockSpec((1, tm, d), lambda b, t, j: (b, t, 0))]
        + _mod_specs(layer, ctx_row, 3, 4, d, 3)
        + [pl.BlockSpec((1, 1, 1, d), lambda b, t, j: (layer, b, 0, 5)),
           pl.BlockSpec((1, 1, 1, d), lambda b, t, j: (layer, ctx_row, 0, 5)),
           pl.BlockSpec((1, d), lambda b, t, j: (0, 0)),
           pl.BlockSpec((d, tf), lambda b, t, j: (0, j)),
           pl.BlockSpec((tf, d), lambda b, t, j: (j, 0))],
        out_specs=pl.BlockSpec((1, tm, d), lambda b, t, j: (b, t, 0)),
        out_shape=jax.ShapeDtypeStruct((bsz, seq, d), F32),
        scratch_shapes=[pltpu.VMEM((tm, d), BF16), pltpu.VMEM((tm, d), F32)],
        compiler_params=pltpu.CompilerParams(dimension_semantics=("parallel", "parallel", "arbitrary")),
        name="mlp",
    )(x, mod, mod, mod, mod, mod, mod, g.reshape(1, d), w1_bf, w2_bf)


def kernel(x, c, ctx, c_ctx, ada_w, ada_b, norm1_g, norm2_g, w_in, q_norm_g, k_norm_g, hg_lb_raw, hg_norm_g,
           hy_short_w, hy_short_b, hy_filt_w1, hy_filt_b1, hy_filt_w2, hy_filt_b2, hy_filt_w3, hy_freq, hy_bias,
           w_branch, w_out, w_mlp1, w_mlp2):
    bsz, n_lat, d = x.shape
    n_ctx = ctx.shape[1]
    depth = ada_w.shape[0]
    gate_cols = 3 * d

    cvec = jnp.concatenate([c, c_ctx[None, :]], axis=0)
    ctx_row = bsz
    pad_rows = -cvec.shape[0] % 8
    cvec = jnp.pad(cvec, ((0, pad_rows), (0, 0)))
    mod = _adaln(cvec, ada_w, ada_b)
    mod = mod.reshape(depth, cvec.shape[0], 1, 6 * d)

    cos, sin_signed = _rope_tables(n_lat, n_ctx)
    dft_ctx = _dft_matrices(n_ctx)
    filt_lat, mats, mats_t = _radix8_matrices(n_lat)

    xs = jnp.concatenate([x, ctx], axis=1)
    for l in range(depth):
        need_ctx = l < depth - 1
        w_l = w_in[l]
        w_bf = jnp.concatenate([w_l[:, -gate_cols:], w_l[:, :-gate_cols]], axis=1).astype(BF16)
        z = _inproj(xs, mod, l, ctx_row, norm1_g[l], w_bf, n_lat)

        att = _attention(z, cos, sin_signed, q_norm_g[l], k_norm_g[l], n_lat, need_ctx)
        hg = _hgrn(z, hg_lb_raw, hg_norm_g[l], l, n_lat, n_ctx)
        w1p = jnp.pad(hy_filt_w1[l], ((0, LANES - HY_EMB), (0, 0)))
        filt_args = (w1p, hy_filt_b1[l], hy_filt_w2[l], hy_filt_b2[l], hy_filt_w3[l], hy_freq[l])
        hy = _hyena_radix8(z, hy_short_w[l], hy_short_b[l], _hyena_filters(n_lat, filt_lat, *filt_args),
                           mats, mats_t, hy_bias[l], n_lat)
        if need_ctx:
            hy_ctx = _hyena_dense(z, hy_short_w[l], hy_short_b[l], _hyena_filters(n_ctx, dft_ctx[0], *filt_args),
                                  dft_ctx, hy_bias[l], n_ctx, n_lat // n_ctx)
            hy = jnp.concatenate([hy, hy_ctx], axis=1)
        rows = n_lat + n_ctx if need_ctx else n_lat
        xs = _merge(xs, att, hg, hy, z, mod, l, ctx_row, w_branch[l].astype(BF16), w_out[l].astype(BF16), n_lat,
                    rows)
        xs = _mlp(xs, mod, l, ctx_row, norm2_g[l], w_mlp1[l].astype(BF16), w_mlp2[l].astype(BF16), n_lat)
    return xs
```

```python
import functools
import math

import jax
import jax.numpy as jnp
import numpy as np
from jax import lax
from jax.experimental import pallas as pl
from jax.experimental.pallas import tpu as pltpu

F32 = jnp.float32
BF16 = jnp.bfloat16

LANES = 128
HEAD_DIM = 128
ATT_HEADS = 8
ATT_KV_HEADS = 2
ATT_GROUP = ATT_HEADS // ATT_KV_HEADS
ROPE_THETA = 10000.0
GRID_W = 64
HG_HEADS = 8
HG_CHUNK = 64
HG_MIN_F = 1e-6
HG_HEADS_PER_STEP = 4
HG_CHAIN_GROUP = 8
HY_BANDS = 16
HY_EMB = 1 + 2 * HY_BANDS
HY_FAST_DECAY = 0.3
HY_SLOW_DECAY = 1.5
HY_TARGET = 1e-2
HY_MIN_DECAY = math.log(HY_TARGET) / HY_SLOW_DECAY
HY_MAX_DECAY = math.log(HY_TARGET) / HY_FAST_DECAY
EPS = 1e-6
DFT_P = 256
HY_FWD_ROWS = 1024
HY_INV_ROWS = 512
HY_TILE = 512
HY_LANE_GROUP = 256
HIGHEST = lax.Precision.HIGHEST

COL_GATES = 0
COL_ATT_Q = 24
COL_ATT_K = 32
COL_ATT_V = 34
COL_HG_Q = 36
COL_HG_F = 44
COL_HG_B = 52
COL_HG_I = 60
COL_HG_G = 68
COL_HY = 76
N_COLS = 100


def _pick(n, candidates):
    for c in candidates:
        if n % c == 0:
            return c
    raise ValueError(f"no tile in {candidates} divides {n}")


def _nt_dot(a, b):
    return lax.dot_general(a, b, (((1,), (1,)), ((), ())), preferred_element_type=F32)


def _adaln_kernel(c_ref, w_ref, b_ref, o_ref):
    cv = c_ref[...]
    s = cv * jax.nn.sigmoid(cv)
    o_ref[0] = jnp.dot(s, w_ref[0], precision=HIGHEST, preferred_element_type=F32) + b_ref[0]


def _adaln(cvec, ada_w, ada_b):
    depth, d, n6 = ada_w.shape
    rows = cvec.shape[0]
    tn = _pick(n6, (1536, 1024, 512, 128))
    return pl.pallas_call(
        _adaln_kernel,
        grid=(depth, n6 // tn),
        in_specs=[pl.BlockSpec((rows, d), lambda l, j: (0, 0)),
                  pl.BlockSpec((1, d, tn), lambda l, j: (l, 0, j)),
                  pl.BlockSpec((1, 1, tn), lambda l, j: (l, 0, j))],
        out_specs=pl.BlockSpec((1, rows, tn), lambda l, j: (l, 0, j)),
        out_shape=jax.ShapeDtypeStruct((depth, rows, n6), F32),
        name="adaln",
    )(cvec, ada_w, ada_b.reshape(depth, 1, n6))


def _modulated_norm(x, g, sh_l, sc_l, sh_c, sc_c, row0, n_lat):
    y = x * lax.rsqrt(jnp.mean(x * x, axis=-1, keepdims=True) + EPS) * g
    row = row0 + lax.broadcasted_iota(jnp.int32, (x.shape[0], 1), 0)
    is_ctx = row >= n_lat
    sc = jnp.where(is_ctx, sc_c, sc_l)
    sh = jnp.where(is_ctx, sh_c, sh_l)
    return y * (1.0 + sc) + sh


def _mod_specs(layer, ctx_row, k_shift, k_scale, d, nargs):
    def spec(row_fn, k):
        if nargs == 3:
            return pl.BlockSpec((1, 1, 1, d), lambda b, t, j: (layer, row_fn(b), 0, k))
        return pl.BlockSpec((1, 1, 1, d), lambda b, t: (layer, row_fn(b), 0, k))
    lat = lambda b: b
    ctx = lambda b: ctx_row
    return [spec(lat, k_shift), spec(lat, k_scale), spec(ctx, k_shift), spec(ctx, k_scale)]


def _inproj_kernel(x_ref, shl_ref, scl_ref, shc_ref, scc_ref, g_ref, w_ref, o_ref, h_ref, *, n_lat, tm):
    t = pl.program_id(1)

    @pl.when(pl.program_id(2) == 0)
    def _():
        h = _modulated_norm(x_ref[0], g_ref[...], shl_ref[0, 0], scl_ref[0, 0], shc_ref[0, 0], scc_ref[0, 0],
                            t * tm, n_lat)
        h_ref[...] = h.astype(BF16)

    o_ref[0] = jnp.dot(h_ref[...], w_ref[...], preferred_element_type=F32).astype(BF16)


def _inproj(x, mod, layer, ctx_row, g, w_bf, n_lat):
    bsz, seq, d = x.shape
    n = w_bf.shape[1]
    tm = _pick(seq, (768, 512, 256))
    tn = _pick(n, (3200, 2560, 1280, 640, 128))
    return pl.pallas_call(
        functools.partial(_inproj_kernel, n_lat=n_lat, tm=tm),
        grid=(bsz, seq // tm, n // tn),
        in_specs=[pl.BlockSpec((1, tm, d), lambda b, t, j: (b, t, 0))]
        + _mod_specs(layer, ctx_row, 0, 1, d, 3)
        + [pl.BlockSpec((1, d), lambda b, t, j: (0, 0)),
           pl.BlockSpec((d, tn), lambda b, t, j: (0, j))],
        out_specs=pl.BlockSpec((1, tm, tn), lambda b, t, j: (b, t, j)),
        out_shape=jax.ShapeDtypeStruct((bsz, seq, n), BF16),
        scratch_shapes=[pltpu.VMEM((tm, d), BF16)],
        compiler_params=pltpu.CompilerParams(dimension_semantics=("parallel", "parallel", "arbitrary")),
        name="inproj",
    )(x, mod, mod, mod, mod, g.reshape(1, d), w_bf)


def _rope(x, cos, sin_signed):
    lane = lax.broadcasted_iota(jnp.int32, x.shape, 1)
    first = (lane % 64) < 32
    rx = jnp.where(first, pltpu.roll(x, 96, 1), pltpu.roll(x, 32, 1))
    return x * cos + rx * sin_signed


def _head_norm(x, g):
    return x * lax.rsqrt(jnp.mean(x * x, axis=-1, keepdims=True) + EPS) * g


def _attn_kernel(q_ref, k_ref, v_ref, cq_ref, sq_ref, ck_ref, sk_ref, gq_ref, gk_ref, o_ref, ks_ref, vs_ref,
                 *, n_lat, tq):
    qi = pl.program_id(2)

    @pl.when(qi == 0)
    def _():
        k = _head_norm(k_ref[0].astype(F32), gk_ref[...])
        ks_ref[...] = _rope(k, ck_ref[...], sk_ref[...]).astype(BF16)
        vs_ref[:, :HEAD_DIM] = v_ref[0]
        vs_ref[:, HEAD_DIM:] = jnp.ones((vs_ref.shape[0], HEAD_DIM), BF16)

    scale = HEAD_DIM ** -0.5 * math.log2(math.e)

    def attend(keys, vals):
        qall = q_ref[0].astype(F32)
        qs = []
        for g in range(ATT_GROUP):
            qg = _head_norm(qall[:, g * HEAD_DIM:(g + 1) * HEAD_DIM], gq_ref[...])
            qs.append((_rope(qg, cq_ref[...], sq_ref[...]) * scale).astype(BF16))
        ss = [_nt_dot(q, keys) for q in qs]
        ps = [jnp.exp2(s - jnp.max(s, axis=-1, keepdims=True)).astype(BF16) for s in ss]
        for g, p in enumerate(ps):
            o2 = jnp.dot(p, vals, preferred_element_type=F32)
            o = o2[:, :HEAD_DIM] / o2[:, HEAD_DIM:]
            o_ref[0, :, g * HEAD_DIM:(g + 1) * HEAD_DIM] = o.astype(BF16)

    is_lat = qi * tq < n_lat

    @pl.when(is_lat)
    def _():
        attend(ks_ref[...], vs_ref[...])

    @pl.when(jnp.logical_not(is_lat))
    def _():
        attend(ks_ref[n_lat:, :], vs_ref[n_lat:, :])


def _attention(z, cos, sin_signed, gq, gk, n_lat, need_ctx):
    bsz, seq, _ = z.shape
    tq = 256
    n_q = (seq if need_ctx else n_lat) // tq
    gw = ATT_GROUP * HEAD_DIM
    return pl.pallas_call(
        functools.partial(_attn_kernel, n_lat=n_lat, tq=tq),
        grid=(bsz, ATT_KV_HEADS, n_q),
        in_specs=[pl.BlockSpec((1, tq, gw), lambda b, h, i: (b, i, COL_ATT_Q // ATT_GROUP + h)),
                  pl.BlockSpec((1, seq, HEAD_DIM), lambda b, h, i: (b, 0, COL_ATT_K + h)),
                  pl.BlockSpec((1, seq, HEAD_DIM), lambda b, h, i: (b, 0, COL_ATT_V + h)),
                  pl.BlockSpec((tq, HEAD_DIM), lambda b, h, i: (i, 0)),
                  pl.BlockSpec((tq, HEAD_DIM), lambda b, h, i: (i, 0)),
                  pl.BlockSpec((seq, HEAD_DIM), lambda b, h, i: (0, 0)),
                  pl.BlockSpec((seq, HEAD_DIM), lambda b, h, i: (0, 0)),
                  pl.BlockSpec((1, HEAD_DIM), lambda b, h, i: (0, 0)),
                  pl.BlockSpec((1, HEAD_DIM), lambda b, h, i: (0, 0))],
        out_specs=pl.BlockSpec((1, tq, gw), lambda b, h, i: (b, i, h)),
        out_shape=jax.ShapeDtypeStruct((bsz, n_q * tq, ATT_HEADS * HEAD_DIM), BF16),
        scratch_shapes=[pltpu.VMEM((seq, HEAD_DIM), BF16), pltpu.VMEM((seq, 2 * HEAD_DIM), BF16)],
        compiler_params=pltpu.CompilerParams(dimension_semantics=("parallel", "parallel", "arbitrary")),
        name="attention",
    )(z, z, z, cos, sin_signed, cos, sin_signed, gq.reshape(1, HEAD_DIM), gk.reshape(1, HEAD_DIM))


def _rope_tables(n_lat, n_ctx):
    half = HEAD_DIM // 2
    inv = ROPE_THETA ** (-np.arange(0, half, 2, dtype=np.float64) / half)
    t = np.arange(n_lat)
    ar = (t // GRID_W)[:, None] * inv
    ac = (t % GRID_W)[:, None] * inv
    ang = np.concatenate([ar, ar, ac, ac], axis=-1)
    cos = np.concatenate([np.cos(ang), np.ones((n_ctx, HEAD_DIM))], axis=0)
    sin = np.concatenate([np.sin(ang), np.zeros((n_ctx, HEAD_DIM))], axis=0)
    sign = np.where((np.arange(HEAD_DIM) % 64) < 32, -1.0, 1.0)
    return jnp.asarray(cos, F32), jnp.asarray(sin * sign, F32)


def _hgrn_masks():
    c = HG_CHUNK
    idx = np.arange(c)
    t, s = idx[:, None], idx[None, :]

    def halving(size):
        half = size // 2
        return (t // size == s // size) & (t % size >= half) & (s % size < half)

    diag = (t // 8 == s // 8) & (s <= t)
    flip = lambda m: m[::-1, ::-1]
    fwd = np.stack([np.concatenate([halving(64), halving(32)], axis=1),
                    np.concatenate([halving(16), diag], axis=1)])
    bwd = np.stack([np.concatenate([flip(halving(64)), flip(halving(32))], axis=1),
                    np.concatenate([flip(halving(16)), flip(diag)], axis=1)])
    return jnp.asarray(np.stack([fwd, bwd]), F32)


def _hgrn_exponents(zg, lb, reverse):
    tile = 8
    nt = HG_CHUNK // tile
    sig = jax.nn.sigmoid(zg.astype(F32))
    f = sig if lb is None else lb + (1.0 - lb) * sig
    lf = jnp.log2(jnp.maximum(f, HG_MIN_F))
    kk = 1.0 - f
    p = lf.reshape(nt, tile, LANES)
    sub = lax.broadcasted_iota(jnp.int32, p.shape, 1)
    for k in (1, 2, 4):
        if reverse:
            p = p + jnp.where(sub < tile - k, pltpu.roll(p, tile - k, 1), 0.0)
        else:
            p = p + jnp.where(sub >= k, pltpu.roll(p, k, 1), 0.0)
    order = list(range(nt - 1, -1, -1)) if reverse else list(range(nt))
    tot_row, mid_row = (0, 4) if reverse else (tile - 1, 3)
    s = [p[j] for j in order]
    tb = [jnp.broadcast_to(p[j, tot_row:tot_row + 1, :], (tile, LANES)) for j in order]
    mb = [jnp.broadcast_to(p[j, mid_row:mid_row + 1, :], (tile, LANES)) for j in order]
    t01, t23, t45 = tb[0] + tb[1], tb[2] + tb[3], tb[4] + tb[5]
    t123, t456 = tb[1] + t23, t45 + tb[6]
    c4 = t01 + t23
    before = [None, tb[0], t01, t01 + tb[2], c4, c4 + tb[4], c4 + t45, c4 + t456]
    total = before[7] + tb[7]
    cum = [s[0]] + [before[k] + s[k] for k in range(1, nt)]
    suf = [total - cum[k] for k in range(nt)]
    w8 = [s[k] - mb[k] for k in range(nt)]
    w16 = [s[k] if k % 2 else tb[k] - s[k] for k in range(nt)]
    w32 = [t01 - s[0], tb[1] - s[1], s[2], tb[2] + s[3], t45 - s[4], tb[5] - s[5], s[6], tb[6] + s[7]]
    w64 = [c4 - s[0], t123 - s[1], t23 - s[2], tb[3] - s[3], s[4], tb[4] + s[5], t45 + s[6], t456 + s[7]]

    def natural(parts):
        return jnp.concatenate([parts[order.index(j)] for j in range(nt)], axis=0)

    ex = {"cum": natural(cum), "suf": natural(suf), 64: natural(w64), 32: natural(w32), 16: natural(w16),
          8: natural(w8), "tot": total[0:1]}
    return kk, ex


def _hgrn_scores(q, kk, ex, st):
    qf = q.astype(F32)
    inter = _nt_dot((qf * jnp.exp2(ex["cum"])).astype(BF16), st.astype(BF16))
    zeros = jnp.zeros(q.shape, BF16)
    pairs = []
    for la, lb_ in ((64, 32), (16, 8)):
        ea, eb = jnp.exp2(ex[la]), jnp.exp2(ex[lb_])
        eb_k = jnp.exp2(-ex[lb_]) if lb_ == 8 else eb
        qa = jnp.concatenate([(qf * ea).astype(BF16), (qf * eb).astype(BF16)], axis=1)
        ka = jnp.concatenate([(kk * ea).astype(BF16), zeros], axis=1)
        kb = jnp.concatenate([zeros, (kk * eb_k).astype(BF16)], axis=1)
        pairs.append(_nt_dot(qa, jnp.concatenate([ka, kb], axis=0)))
    return inter, pairs


def _hgrn_output(inter, pairs, v, mask):
    p0 = pairs[0] * mask[0]
    p1 = jnp.where(mask[1] > 0.0, pairs[1], 0.0)
    probs = jnp.concatenate([p0.astype(BF16), p1.astype(BF16)], axis=1)
    return inter + jnp.dot(probs, jnp.concatenate([v, v, v, v], axis=0), preferred_element_type=F32)


def _hgrn_state(kk, ex, v, st):
    k_suf = (kk * jnp.exp2(ex["suf"])).astype(BF16)
    upd = lax.dot_general(v, k_suf, (((0,), (0,)), ((), ())), preferred_element_type=F32)
    return st * jnp.exp2(ex["tot"]) + upd


def _hgrn_kernel(q_ref, zf_ref, zb_ref, i_ref, g_ref, lbraw_ref, gn_ref, mask_ref, o_ref,
                 acc_ref, st_ref, *, layer, n_lat, n_ctx):
    c = HG_CHUNK
    nc_lat, nc_ctx = n_lat // c, n_ctx // c
    nc = nc_lat + nc_ctx
    raw = lbraw_ref[...].astype(F32)
    ew = jnp.exp(raw - jnp.max(raw, axis=0, keepdims=True))
    sm = ew / jnp.sum(ew, axis=0, keepdims=True)
    lower = jnp.zeros_like(sm[0])
    for dpt in range(1, layer + 1):
        lower = lower + sm[dpt]

    acc_ref[...] = jnp.zeros_like(acc_ref)
    st_ref[...] = jnp.zeros_like(st_ref)

    def step(n, carry):
        cf = jnp.where(n < nc_ctx, nc_lat + n, n - nc_ctx)
        cb = nc - 1 - n
        chains = []
        for hh in range(HG_HEADS_PER_STEP):
            lanes = slice(hh * LANES, (hh + 1) * LANES)
            for d, (chunk, z_ref) in enumerate(((cf, zf_ref), (cb, zb_ref))):
                rows = pl.ds(pl.multiple_of(chunk * c, c), c)
                chains.append((hh, d, rows, lanes, z_ref))
        for g0 in range(0, len(chains), HG_CHAIN_GROUP):
            group = chains[g0:g0 + HG_CHAIN_GROUP]
            gates = [_hgrn_exponents(z_ref[0, rows, lanes], None if layer == 0 else lower[hh, d:d + 1], d == 1)
                     for hh, d, rows, lanes, z_ref in group]
            scored = [_hgrn_scores(q_ref[0, rows, lanes], kk, ex, st_ref[hh, d])
                      for (hh, d, rows, lanes, _), (kk, ex) in zip(group, gates)]
            for (hh, d, rows, lanes, _), (inter, pairs) in zip(group, scored):
                out = _hgrn_output(inter, pairs, i_ref[0, rows, lanes], mask_ref[d])
                acc_ref[rows, lanes] = acc_ref[rows, lanes] + out
            for (hh, d, rows, lanes, _), (kk, ex) in zip(group, gates):
                st_ref[hh, d] = _hgrn_state(kk, ex, i_ref[0, rows, lanes], st_ref[hh, d])
        return carry

    lax.fori_loop(0, nc, step, 0)

    for hh in range(HG_HEADS_PER_STEP):
        lanes = slice(hh * LANES, (hh + 1) * LANES)
        o = acc_ref[:, lanes]
        y = o * lax.rsqrt(jnp.mean(o * o, axis=-1, keepdims=True) + EPS) * gn_ref[...]
        g = g_ref[0, :, lanes].astype(F32)
        o_ref[0, :, lanes] = (y * (g * jax.nn.sigmoid(g))).astype(BF16)


def _hgrn(z, lb_raw, g_norm, layer, n_lat, n_ctx):
    bsz, seq, _ = z.shape
    depth = lb_raw.shape[0]
    hps = HG_HEADS_PER_STEP
    width = hps * LANES
    masks = _hgrn_masks()
    lbr = lb_raw.reshape(depth, 2, HG_HEADS, LANES).transpose(0, 2, 1, 3)
    col = lambda c0: pl.BlockSpec((1, seq, width), lambda b, h: (b, 0, c0 // hps + h))
    return pl.pallas_call(
        functools.partial(_hgrn_kernel, layer=layer, n_lat=n_lat, n_ctx=n_ctx),
        grid=(bsz, HG_HEADS // hps),
        in_specs=[col(COL_HG_Q), col(COL_HG_F), col(COL_HG_B), col(COL_HG_I), col(COL_HG_G),
                  pl.BlockSpec((depth, hps, 2, LANES), lambda b, h: (0, h, 0, 0)),
                  pl.BlockSpec((1, LANES), lambda b, h: (0, 0)),
                  pl.BlockSpec((2, 2, HG_CHUNK, LANES), lambda b, h: (0, 0, 0, 0))],
        out_specs=pl.BlockSpec((1, seq, width), lambda b, h: (b, 0, h)),
        out_shape=jax.ShapeDtypeStruct((bsz, seq, HG_HEADS * LANES), BF16),
        scratch_shapes=[pltpu.VMEM((seq, width), F32), pltpu.VMEM((hps, 2, LANES, LANES), F32)],
        compiler_params=pltpu.CompilerParams(dimension_semantics=("parallel", "parallel")),
        name="hgrn",
    )(z, z, z, z, z, lbr, g_norm.reshape(1, LANES), masks)


def _dft_matrices(n):
    size = 2 * n
    k = np.arange(n)[:, None]
    t = np.arange(n)[None, :]
    ang = (2.0 * np.pi / size) * ((k * t) % size)
    cos, sin = np.cos(ang), np.sin(ang)
    sin[0, :] = np.where(np.arange(n) % 2 == 0, 1.0, -1.0)
    fwd = np.concatenate([cos.reshape(n // DFT_P, DFT_P, n), sin.reshape(n // DFT_P, DFT_P, n)], axis=1)
    fwd = fwd.reshape(size, n)
    pairs = [(cos[i:i + DFT_P], sin[i:i + DFT_P]) for i in range(0, n, DFT_P)]
    filt = _group_filter_rows(pairs, _filter_group(n))
    return jnp.asarray(filt, BF16), (jnp.asarray(fwd, BF16), jnp.asarray(fwd.T.copy(), BF16))


def _hyena_positions(n):
    t = np.linspace(0.0, 1.0, n)[:, None]
    w = (2.0 * math.pi / n) * np.arange(n)[:, None]
    f = np.linspace(1e-4, HY_BANDS - 1, HY_BANDS)[None, :]
    z = np.concatenate([t, np.cos(f * w), -np.sin(f * w)], axis=-1)
    return jnp.asarray(np.pad(z, ((0, 0), (0, LANES - HY_EMB))), F32)


def _filter_kernel(pos_ref, w1_ref, b1_ref, w2_ref, b2_ref, fr_ref, w3f_ref, w3b_ref, dl_ref, fs_ref, fd_ref,
                   h_ref):
    pos = pos_ref[...]

    @pl.when((pl.program_id(0) == 0) & (pl.program_id(1) == 0))
    def _():
        h1 = jnp.sin(fr_ref[0:1] * (jnp.dot(pos, w1_ref[...], precision=HIGHEST, preferred_element_type=F32)
                                    + b1_ref[...]))
        h_ref[...] = jnp.sin(fr_ref[1:2] * (jnp.dot(h1, w2_ref[...], precision=HIGHEST,
                                                    preferred_element_type=F32) + b2_ref[...]))

    h = h_ref[...]
    window = jnp.exp(-pos[:, 0:1] * dl_ref[...])
    hf = jnp.dot(h, w3f_ref[...], precision=HIGHEST, preferred_element_type=F32) * window
    hb = jnp.dot(h, w3b_ref[...], precision=HIGHEST, preferred_element_type=F32) * window
    row = lax.broadcasted_iota(jnp.int32, (pos.shape[0], 1), 0)
    hb = jnp.where(row == 0, 0.0, hb)
    norm = jnp.sum(jnp.abs(hf), axis=0, keepdims=True) + jnp.sum(jnp.abs(hb), axis=0, keepdims=True) + EPS
    inv = 1.0 / norm
    fs_ref[0] = ((hf + hb) * inv).astype(BF16)
    fd_ref[0] = ((hf - hb) * inv).astype(BF16)


def _filter_group(n):
    chunks = n // DFT_P
    return 4 if chunks % 4 == 0 else 1


def _group_filter_rows(pairs, group):
    out = []
    for g0 in range(0, len(pairs), group):
        out += [c for c, _ in pairs[g0:g0 + group]] + [s for _, s in pairs[g0:g0 + group]]
    return np.concatenate(out)


def _filter_dft_kernel(a_ref, fs_ref, fd_ref, o_ref, *, n, group):
    m = pl.program_id(2)
    p = DFT_P
    gp = group * p
    first = (lax.broadcasted_iota(jnp.int32, (gp, 1), 0) == 0) & (m == 0)
    weight = jnp.where(first, 1.0, 2.0) / (2 * n)
    hc = jnp.dot(a_ref[:gp], fs_ref[0], preferred_element_type=F32) * weight
    hs = jnp.dot(a_ref[gp:], fd_ref[0], preferred_element_type=F32) * weight
    for g in range(group):
        o_ref[0, 2 * p * g:2 * p * g + p] = hc[g * p:(g + 1) * p]
        o_ref[0, 2 * p * g + p:2 * p * (g + 1)] = hs[g * p:(g + 1) * p]

    @pl.when(m == 0)
    def _():
        hny = jnp.dot(a_ref[gp:gp + 16], fs_ref[0], preferred_element_type=F32)[:8]
        o_ref[0, p:p + 8] = jnp.where(first[:8], hny / (2 * n), hs[:8])


def _hyena_filters(n, fwd, w1p, b1, w2, b2, w3, freq):
    hid = w2.shape[0]
    c = w3.shape[1] // 4
    tn = _pick(c, (512, 256, 128))
    nct = c // tn
    pos = _hyena_positions(n)
    deltas = jnp.asarray(np.abs(np.linspace(HY_MIN_DECAY, HY_MAX_DECAY, c))[None, :], F32)
    full = lambda shape: pl.BlockSpec(shape, lambda o, j: (0,) * len(shape))
    fsum, fdiff = pl.pallas_call(
        _filter_kernel,
        grid=(2, nct),
        in_specs=[full((n, LANES)), full((LANES, hid)), full((1, hid)), full((hid, hid)), full((1, hid)),
                  full((2, hid)),
                  pl.BlockSpec((hid, tn), lambda o, j: (0, (2 * o) * nct + j)),
                  pl.BlockSpec((hid, tn), lambda o, j: (0, (2 * o + 1) * nct + j)),
                  pl.BlockSpec((1, tn), lambda o, j: (0, j))],
        out_specs=[pl.BlockSpec((1, n, tn), lambda o, j: (o, 0, j))] * 2,
        out_shape=[jax.ShapeDtypeStruct((2, n, c), BF16)] * 2,
        scratch_shapes=[pltpu.VMEM((n, hid), F32)],
        compiler_params=pltpu.CompilerParams(dimension_semantics=("arbitrary", "arbitrary")),
        name="hyena_filter",
    )(pos, w1p, b1.reshape(1, hid), w2, b2.reshape(1, hid), freq, w3, w3, deltas)
    group = _filter_group(n)
    rows = 2 * DFT_P * group
    return pl.pallas_call(
        functools.partial(_filter_dft_kernel, n=n, group=group),
        grid=(2, nct, 2 * n // rows),
        in_specs=[pl.BlockSpec((rows, n), lambda o, j, m: (m, 0)),
                  pl.BlockSpec((1, n, tn), lambda o, j, m: (o, 0, j)),
                  pl.BlockSpec((1, n, tn), lambda o, j, m: (o, 0, j))],
        out_specs=pl.BlockSpec((1, rows, tn), lambda o, j, m: (o, m, j)),
        out_shape=jax.ShapeDtypeStruct((2, 2 * n, c), F32),
        name="hyena_filter_dft",
    )(fwd, fsum, fdiff)


def _short_conv(u, w, b):
    n = u.shape[0]
    row = lax.broadcasted_iota(jnp.int32, (n, 1), 0)
    prev = jnp.where(row == 0, 0.0, pltpu.roll(u, 1, 0))
    nxt = jnp.where(row == n - 1, 0.0, pltpu.roll(u, n - 1, 0))
    return prev * w[0:1] + u * w[1:2] + nxt * w[2:3] + b


def _shortconv_kernel(u_ref, w_ref, b_ref, o_ref):
    o_ref[0] = _short_conv(u_ref[0].astype(F32), w_ref[...], b_ref[...]).astype(BF16)


def _shortconv(z, w, b, n, row_blk):
    bsz = z.shape[0]
    width = w.shape[1]
    tn = HY_TILE
    assert (COL_HY * LANES) % tn == 0 and width % tn == 0
    c0 = COL_HY * LANES // tn
    return pl.pallas_call(
        _shortconv_kernel,
        grid=(bsz, width // tn),
        in_specs=[pl.BlockSpec((1, n, tn), lambda bb, j: (bb, row_blk, c0 + j)),
                  pl.BlockSpec((3, tn), lambda bb, j: (0, j)),
                  pl.BlockSpec((1, tn), lambda bb, j: (0, j))],
        out_specs=pl.BlockSpec((1, n, tn), lambda bb, j: (bb, 0, j)),
        out_shape=jax.ShapeDtypeStruct((bsz, n, width), BF16),
        compiler_params=pltpu.CompilerParams(dimension_semantics=("parallel", "parallel")),
        name="hyena_shortconv",
    )(z, w, b.reshape(1, width))


def _conv_fwd_kernel(a_ref, v_ref, h_ref, y_ref, *, chunks):
    p = DFT_P
    zf = jnp.dot(a_ref[...], v_ref[0], preferred_element_type=F32)
    for k in range(chunks):
        lo = 2 * p * k
        zc, zs = zf[lo:lo + p], zf[lo + p:lo + 2 * p]
        hc, hs = h_ref[0, lo:lo + p], h_ref[0, lo + p:lo + 2 * p]
        yc, ys = zc * hc - zs * hs, zc * hs + zs * hc
        if k == 0:
            real_row = (lax.broadcasted_iota(jnp.int32, (p, 1), 0) == 0) & (pl.program_id(1) == 0)
            yc = jnp.where(real_row, zc * hc, yc)
            ys = jnp.where(real_row, zs * hs, ys)
        y_ref[0, lo:lo + p] = yc.astype(BF16)
        y_ref[0, lo + p:lo + 2 * p] = ys.astype(BF16)


def _conv_fwd(fwd, v_arr, v_col, spec, order, n, row_blk):
    bsz = v_arr.shape[0]
    c = spec.shape[2]
    rows = min(HY_FWD_ROWS, 2 * n)
    return pl.pallas_call(
        functools.partial(_conv_fwd_kernel, chunks=rows // (2 * DFT_P)),
        grid=(bsz, 2 * n // rows),
        in_specs=[pl.BlockSpec((rows, n), lambda b, m: (m, 0)),
                  pl.BlockSpec((1, n, c), lambda b, m: (b, row_blk, v_col)),
                  pl.BlockSpec((1, rows, c), lambda b, m: (order, m, 0))],
        out_specs=pl.BlockSpec((1, rows, c), lambda b, m: (b, m, 0)),
        out_shape=jax.ShapeDtypeStruct((bsz, 2 * n, c), BF16),
        compiler_params=pltpu.CompilerParams(dimension_semantics=("parallel", "arbitrary")),
        name="hyena_conv_fwd",
    )(fwd, v_arr, spec)


def _conv_inv_kernel(g_ref, y_ref, gate_ref, zin_ref, bias_ref, o_ref):
    y = jnp.dot(g_ref[...], y_ref[0], preferred_element_type=F32)
    o_ref[0] = (gate_ref[0].astype(F32) * (y + zin_ref[0].astype(F32) * bias_ref[0])).astype(BF16)


def _conv_inv(inv, y, gate, zin, bias, order, n):
    bsz, _, c = y.shape
    tm = min(HY_INV_ROWS, n)

    def rows_of(triple):
        _, row0, col = triple
        return pl.BlockSpec((1, tm, c), lambda b, m: (b, row0 // tm + m, col))

    return pl.pallas_call(
        _conv_inv_kernel,
        grid=(bsz, n // tm),
        in_specs=[pl.BlockSpec((tm, 2 * n), lambda b, m: (m, 0)),
                  pl.BlockSpec((1, 2 * n, c), lambda b, m: (b, 0, 0)),
                  rows_of(gate), rows_of(zin),
                  pl.BlockSpec((1, 1, c), lambda b, m: (order, 0, 0))],
        out_specs=pl.BlockSpec((1, tm, c), lambda b, m: (b, m, 0)),
        out_shape=jax.ShapeDtypeStruct((bsz, n, c), BF16),
        compiler_params=pltpu.CompilerParams(dimension_semantics=("parallel", "arbitrary")),
        name="hyena_conv_inv",
    )(inv, y, gate[0], zin[0], bias.reshape(2, 1, c))


def _hyena_dense(z, sw, sb, spectrum, dft, bias, n, row_blk):
    u = _shortconv(z, sw, sb, n, row_blk)
    x1_col, x2_col, v_col = 0, 1, 2
    fwd, inv = dft
    y = _conv_fwd(fwd, u, v_col, spectrum, 0, n, 0)
    z1 = _conv_inv(inv, y, (u, 0, x1_col), (u, 0, v_col), bias, 0, n)
    y = _conv_fwd(fwd, z1, 0, spectrum, 1, n, 0)
    return _conv_inv(inv, y, (u, 0, x2_col), (z1, 0, 0), bias, 1, n)


def _radix8_frequencies(n):
    p = n // 8
    j = np.arange(p)
    return [8 * j, 4 + 8 * j, 1 + 8 * j, 1 + 8 * (j + p), 2 + 8 * j, 2 + 8 * (j + p), 3 + 8 * j, 3 + 8 * (j + p)]


def _radix8_matrices(n):
    size, p = 2 * n, n // 8
    freqs = _radix8_frequencies(n)

    def cos_sin(k, length):
        ang = (2.0 * np.pi / size) * ((k[:, None] * np.arange(length)[None, :]) % size)
        return np.cos(ang), np.sin(ang)

    def nyquist(sin_rows, length):
        sin_rows[0, :] = np.where(np.arange(length) % 2 == 0, 1.0, -1.0)
        return sin_rows

    pairs = []
    for idx, k in enumerate(freqs):
        c, s = cos_sin(k, n)
        pairs.append((c, nyquist(s, n) if idx == 0 else s))
    filt = _group_filter_rows(pairs, _filter_group(n))
    slab = size // 8
    c0, s0 = cos_sin(freqs[0], slab)
    c4, s4 = cos_sin(freqs[1], slab)
    mats = [np.concatenate([c0, nyquist(s0, slab)]), np.concatenate([c4, s4])]
    for k1 in (1, 2, 3):
        ca, sa = cos_sin(freqs[2 * k1], slab)
        cb, sb = cos_sin(freqs[2 * k1 + 1], slab)
        mats.append(np.block([[ca, sa], [sa, -ca], [cb, sb], [sb, -cb]]))
    to_bf = lambda a: jnp.asarray(a, BF16)
    return to_bf(filt), [to_bf(m) for m in mats], [to_bf(m.T.copy()) for m in mats]


def _conv_fwd8_kernel(*refs, pre_conv):
    if pre_conv:
        v_ref, w_ref, b_ref, r0_ref, r4_ref, r1_ref, r2_ref, r3_ref, h_ref, y_ref, vs_ref = refs
    else:
        v_ref, r0_ref, r4_ref, r1_ref, r2_ref, r3_ref, h_ref, y_ref = refs
    n, tn = v_ref.shape[1], v_ref.shape[2]
    slab, p = n // 4, n // 8
    bf = lambda x: x.astype(BF16)
    cat = lambda re, im: jnp.concatenate([bf(re), bf(im)], axis=0)
    groups = [slice(g * HY_LANE_GROUP, (g + 1) * HY_LANE_GROUP) for g in range(tn // HY_LANE_GROUP)]
    inputs = []
    for lanes in groups:
        v = v_ref[0, :, lanes].astype(F32)
        if pre_conv:
            v = _short_conv(v, w_ref[:, lanes], b_ref[:, lanes])
            vs_ref[0, :, lanes] = v.astype(BF16)
        z0, z1, z2, z3 = (v[i * slab:(i + 1) * slab] for i in range(4))
        e, o = z0 + z2, z1 + z3
        a, b = (z1 - z3) * (0.5 ** 0.5), o * (0.5 ** 0.5)
        inputs.append((bf(e + o), bf(e - o), cat(z0 + a, -z2 - b), cat(z0 - z2, z3 - z1), cat(z0 - a, z2 - b)))
    spectra = [[jnp.dot(r[...], x, preferred_element_type=F32)
                for r, x in zip((r0_ref, r4_ref, r1_ref, r2_ref, r3_ref), xs)] for xs in inputs]
    for lanes, spec in zip(groups, spectra):
        chunks = [spec[0], spec[1]] + [s[i * 2 * p:(i + 1) * 2 * p] for s in spec[2:] for i in range(2)]
        for c, zf in enumerate(chunks):
            lo = 2 * p * c
            zc, zs = zf[:p], zf[p:]
            hc, hs = h_ref[0, lo:lo + p, lanes], h_ref[0, lo + p:lo + 2 * p, lanes]
            yc, ys = zc * hc - zs * hs, zc * hs + zs * hc
            if c == 0:
                real_row = lax.broadcasted_iota(jnp.int32, (p, 1), 0) == 0
                yc = jnp.where(real_row, zc * hc, yc)
                ys = jnp.where(real_row, zs * hs, ys)
            y_ref[0, lo:lo + p, lanes] = yc.astype(BF16)
            y_ref[0, lo + p:lo + 2 * p, lanes] = ys.astype(BF16)


def _const_spec(shape):
    return pl.BlockSpec(shape, lambda j, b: (0,) * len(shape), pipeline_mode=pl.Buffered(1))


def _conv_fwd8(mats, v_arr, v_col, spec, order, n, short=None):
    bsz = v_arr.shape[0]
    c = spec.shape[2]
    tn = HY_TILE
    in_specs = [pl.BlockSpec((1, n, tn), lambda j, b: (b, 0, v_col + j))]
    args = [v_arr]
    out_specs = [pl.BlockSpec((1, 2 * n, tn), lambda j, b: (b, 0, j))]
    out_shape = [jax.ShapeDtypeStruct((bsz, 2 * n, c), BF16)]
    if short is not None:
        w, bvec, wcol = short
        in_specs += [pl.BlockSpec((3, tn), lambda j, b: (0, wcol + j)),
                     pl.BlockSpec((1, tn), lambda j, b: (0, wcol + j))]
        args += [w, bvec.reshape(1, -1)]
        out_specs.append(pl.BlockSpec((1, n, tn), lambda j, b: (b, 0, j)))
        out_shape.append(jax.ShapeDtypeStruct((bsz, n, c), BF16))
    in_specs += [_const_spec(m.shape) for m in mats]
    in_specs.append(pl.BlockSpec((1, 2 * n, tn), lambda j, b: (order, 0, j)))
    res = pl.pallas_call(
        functools.partial(_conv_fwd8_kernel, pre_conv=short is not None),
        grid=(c // tn, bsz),
        in_specs=in_specs,
        out_specs=out_specs,
        out_shape=out_shape,
        compiler_params=pltpu.CompilerParams(dimension_semantics=("parallel", "parallel")),
        name="hyena_conv_fwd8",
    )(*args, *mats, spec)
    return res if short is not None else res[0]


def _conv_inv8_kernel(y_ref, r0_ref, r4_ref, r1_ref, r2_ref, r3_ref, gate_ref, w_ref, b_ref, zin_ref, bias_ref,
                      o_ref):
    n, tn = o_ref.shape[1], o_ref.shape[2]
    slab = n // 4
    groups = [slice(g * HY_LANE_GROUP, (g + 1) * HY_LANE_GROUP) for g in range(tn // HY_LANE_GROUP)]
    bounds = (0, slab, 2 * slab, 4 * slab, 6 * slab, 8 * slab)
    gates = [_short_conv(gate_ref[0, :, lanes].astype(F32), w_ref[:, lanes], b_ref[:, lanes]) for lanes in groups]
    skips = [zin_ref[0, :, lanes].astype(F32) * bias_ref[0, :, lanes] for lanes in groups]
    parts = [[jnp.dot(r[...], y_ref[0, lo:hi, lanes], preferred_element_type=F32)
              for r, lo, hi in zip((r0_ref, r4_ref, r1_ref, r2_ref, r3_ref), bounds[:-1], bounds[1:])]
             for lanes in groups]
    for lanes, gate, skip, (v0, v4, v1, v2, v3) in zip(groups, gates, skips, parts):
        (v1r, v1i), (v2r, v2i), (v3r, v3i) = ((v[:slab], v[slab:]) for v in (v1, v2, v3))
        s, d = v0 + v4, v0 - v4
        r = 0.5 ** 0.5
        outs = [s + v1r + v2r + v3r,
                d + (v1r - v1i - v3r - v3i) * r - v2i,
                s - v1i - v2r + v3i,
                d + (v3r - v3i - v1r - v1i) * r + v2i]
        for i, y in enumerate(outs):
            rows = slice(i * slab, (i + 1) * slab)
            o_ref[0, rows, lanes] = (gate[rows] * (y + skip[rows])).astype(BF16)


def _conv_inv8(mats_t, y, gate_arr, gate_col, short, zin, bias, order, n):
    bsz, _, c = y.shape
    tn = HY_TILE
    w, bvec, wcol = short
    return pl.pallas_call(
        _conv_inv8_kernel,
        grid=(c // tn, bsz),
        in_specs=[pl.BlockSpec((1, 2 * n, tn), lambda j, b: (b, 0, j))]
        + [_const_spec(m.shape) for m in mats_t]
        + [pl.BlockSpec((1, n, tn), lambda j, b: (b, 0, gate_col + j)),
           pl.BlockSpec((3, tn), lambda j, b: (0, wcol + j)),
           pl.BlockSpec((1, tn), lambda j, b: (0, wcol + j)),
           pl.BlockSpec((1, n, tn), lambda j, b: (b, 0, j)),
           pl.BlockSpec((1, 1, tn), lambda j, b: (order, 0, j))],
        out_specs=pl.BlockSpec((1, n, tn), lambda j, b: (b, 0, j)),
        out_shape=jax.ShapeDtypeStruct((bsz, n, c), BF16),
        compiler_params=pltpu.CompilerParams(dimension_semantics=("parallel", "parallel")),
        name="hyena_conv_inv8",
    )(y, *mats_t, gate_arr, w, bvec.reshape(1, -1), zin, bias.reshape(2, 1, c))


def _hyena_radix8(z, sw, sb, spectrum, mats, mats_t, bias, n):
    c = bias.shape[1]
    per = c // HY_TILE
    z_col = COL_HY * LANES // HY_TILE
    y, vs = _conv_fwd8(mats, z, z_col + 2 * per, spectrum, 0, n, short=(sw, sb, 2 * per))
    z1 = _conv_inv8(mats_t, y, z, z_col, (sw, sb, 0), vs, bias, 0, n)
    y = _conv_fwd8(mats, z1, 0, spectrum, 1, n)
    return _conv_inv8(mats_t, y, z, z_col + per, (sw, sb, per), z1, bias, 1, n)


def _merge_kernel(x_ref, att_ref, hg_ref, hy_ref, ga_ref, gh_ref, gy_ref, gl_ref, gc_ref, wb_ref, wo_ref,
                  o_ref, *, n_lat, tm):
    def branch(gate_ref, val_ref, k):
        gate = jax.nn.sigmoid(gate_ref[0].astype(F32))
        return gate * jnp.dot(val_ref[0], wb_ref[k], preferred_element_type=F32)

    merged = branch(ga_ref, att_ref, 0) + branch(gh_ref, hg_ref, 1) + branch(gy_ref, hy_ref, 2)
    r = jnp.dot(merged.astype(BF16), wo_ref[...], preferred_element_type=F32)
    row = pl.program_id(1) * tm + lax.broadcasted_iota(jnp.int32, (tm, 1), 0)
    gate = jnp.where(row >= n_lat, gc_ref[0, 0], gl_ref[0, 0])
    o_ref[0] = x_ref[0] + gate * r


def _merge(x, att, hg, hy, z, mod, layer, ctx_row, wb_bf, wo_bf, n_lat, rows):
    bsz, _, d = x.shape
    seq = rows
    tm = _pick(seq, (512, 384, 256))
    tok = lambda: pl.BlockSpec((1, tm, d), lambda b, t: (b, t, 0))
    gate = lambda k: pl.BlockSpec((1, tm, d), lambda b, t: (b, t, COL_GATES * LANES // d + k))
    return pl.pallas_call(
        functools.partial(_merge_kernel, n_lat=n_lat, tm=tm),
        grid=(bsz, seq // tm),
        in_specs=[tok(), tok(), tok(), tok(), gate(0), gate(1), gate(2),
                  pl.BlockSpec((1, 1, 1, d), lambda b, t: (layer, b, 0, 2)),
                  pl.BlockSpec((1, 1, 1, d), lambda b, t: (layer, ctx_row, 0, 2)),
                  pl.BlockSpec((3, d, d), lambda b, t: (0, 0, 0)),
                  pl.BlockSpec((d, d), lambda b, t: (0, 0))],
        out_specs=tok(),
        out_shape=jax.ShapeDtypeStruct((bsz, seq, d), F32),
        compiler_params=pltpu.CompilerParams(dimension_semantics=("parallel", "parallel")),
        name="merge",
    )(x, att, hg, hy, z, z, z, mod, mod, wb_bf, wo_bf)


def _mlp_kernel(x_ref, shl_ref, scl_ref, shc_ref, scc_ref, gl_ref, gc_ref, g_ref, w1_ref, w2_ref, o_ref,
                h_ref, acc_ref, *, n_lat, tm):
    t = pl.program_id(1)
    j = pl.program_id(2)

    @pl.when(j == 0)
    def _():
        h = _modulated_norm(x_ref[0], g_ref[...], shl_ref[0, 0], scl_ref[0, 0], shc_ref[0, 0], scc_ref[0, 0],
                            t * tm, n_lat)
        h_ref[...] = h.astype(BF16)
        acc_ref[...] = jnp.zeros_like(acc_ref)

    a = jnp.maximum(jnp.dot(h_ref[...], w1_ref[...], preferred_element_type=F32), 0.0)
    acc_ref[...] += jnp.dot((a * a).astype(BF16), w2_ref[...], preferred_element_type=F32)

    @pl.when(j == pl.num_programs(2) - 1)
    def _():
        row = t * tm + lax.broadcasted_iota(jnp.int32, (tm, 1), 0)
        gate = jnp.where(row >= n_lat, gc_ref[0, 0], gl_ref[0, 0])
        o_ref[0] = x_ref[0] + gate * acc_ref[...]


def _mlp(x, mod, layer, ctx_row, g, w1_bf, w2_bf, n_lat):
    bsz, seq, d = x.shape
    ff = w1_bf.shape[1]
    tm = _pick(seq, (768, 512, 256))
    tf = _pick(ff, (2048, 1024, 512, 128))
    return pl.pallas_call(
        functools.partial(_mlp_kernel, n_lat=n_lat, tm=tm),
        grid=(bsz, seq // tm, ff // tf),
        in_specs=[pl.BlockSpec((1, tm, d), lambda b, t, j: (b, t, 0))]
        + _mod_specs(layer, ctx_row, 3, 4, d, 3)
        + [pl.BlockSpec((1, 1, 1, d), lambda b, t, j: (layer, b, 0, 5)),
           pl.BlockSpec((1, 1, 1, d), lambda b, t, j: (layer, ctx_row, 0, 5)),
           pl.BlockSpec((1, d), lambda b, t, j: (0, 0)),
           pl.BlockSpec((d, tf), lambda b, t, j: (0, j)),
           pl.BlockSpec((tf, d), lambda b, t, j: (j, 0))],
        out_specs=pl.BlockSpec((1, tm, d), lambda b, t, j: (b, t, 0)),
        out_shape=jax.ShapeDtypeStruct((bsz, seq, d), F32),
        scratch_shapes=[pltpu.VMEM((tm, d), BF16), pltpu.VMEM((tm, d), F32)],
        compiler_params=pltpu.CompilerParams(dimension_semantics=("parallel", "parallel", "arbitrary")),
        name="mlp",
    )(x, mod, mod, mod, mod, mod, mod, g.reshape(1, d), w1_bf, w2_bf)


def kernel(x, c, ctx, c_ctx, ada_w, ada_b, norm1_g, norm2_g, w_in, q_norm_g, k_norm_g, hg_lb_raw, hg_norm_g,
           hy_short_w, hy_short_b, hy_filt_w1, hy_filt_b1, hy_filt_w2, hy_filt_b2, hy_filt_w3, hy_freq, hy_bias,
           w_branch, w_out, w_mlp1, w_mlp2):
    bsz, n_lat, d = x.shape
    n_ctx = ctx.shape[1]
    depth = ada_w.shape[0]
    gate_cols = 3 * d

    cvec = jnp.concatenate([c, c_ctx[None, :]], axis=0)
    ctx_row = bsz
    pad_rows = -cvec.shape[0] % 8
    cvec = jnp.pad(cvec, ((0, pad_rows), (0, 0)))
    mod = _adaln(cvec, ada_w, ada_b)
    mod = mod.reshape(depth, cvec.shape[0], 1, 6 * d)

    cos, sin_signed = _rope_tables(n_lat, n_ctx)
    filt_ctx, dft_ctx = _dft_matrices(n_ctx)
    filt_lat, mats, mats_t = _radix8_matrices(n_lat)

    xs = jnp.concatenate([x, ctx], axis=1)
    for l in range(depth):
        need_ctx = l < depth - 1
        w_l = w_in[l]
        w_bf = jnp.concatenate([w_l[:, -gate_cols:], w_l[:, :-gate_cols]], axis=1).astype(BF16)
        z = _inproj(xs, mod, l, ctx_row, norm1_g[l], w_bf, n_lat)

        att = _attention(z, cos, sin_signed, q_norm_g[l], k_norm_g[l], n_lat, need_ctx)
        hg = _hgrn(z, hg_lb_raw, hg_norm_g[l], l, n_lat, n_ctx)
        w1p = jnp.pad(hy_filt_w1[l], ((0, LANES - HY_EMB), (0, 0)))
        filt_args = (w1p, hy_filt_b1[l], hy_filt_w2[l], hy_filt_b2[l], hy_filt_w3[l], hy_freq[l])
        hy = _hyena_radix8(z, hy_short_w[l], hy_short_b[l], _hyena_filters(n_lat, filt_lat, *filt_args),
                           mats, mats_t, hy_bias[l], n_lat)
        if need_ctx:
            hy_ctx = _hyena_dense(z, hy_short_w[l], hy_short_b[l], _hyena_filters(n_ctx, filt_ctx, *filt_args),
                                  dft_ctx, hy_bias[l], n_ctx, n_lat // n_ctx)
            hy = jnp.concatenate([hy, hy_ctx], axis=1)
        rows = n_lat + n_ctx if need_ctx else n_lat
        xs = _merge(xs, att, hg, hy, z, mod, l, ctx_row, w_branch[l].astype(BF16), w_out[l].astype(BF16), n_lat,
                    rows)
        xs = _mlp(xs, mod, l, ctx_row, norm2_g[l], w_mlp1[l].astype(BF16), w_mlp2[l].astype(BF16), n_lat)
    return xs
```

```python
import functools
import math

import jax
import jax.numpy as jnp
import numpy as np
from jax import lax
from jax.experimental import pallas as pl
from jax.experimental.pallas import tpu as pltpu

F32 = jnp.float32
BF16 = jnp.bfloat16

LANES = 128
HEAD_DIM = 128
ATT_HEADS = 8
ATT_KV_HEADS = 2
ATT_GROUP = ATT_HEADS // ATT_KV_HEADS
ROPE_THETA = 10000.0
GRID_W = 64
HG_HEADS = 8
HG_CHUNK = 64
HG_MIN_F = 1e-6
HG_HEADS_PER_STEP = 4
HG_CHAIN_GROUP = 8
HY_BANDS = 16
HY_EMB = 1 + 2 * HY_BANDS
HY_FAST_DECAY = 0.3
HY_SLOW_DECAY = 1.5
HY_TARGET = 1e-2
HY_MIN_DECAY = math.log(HY_TARGET) / HY_SLOW_DECAY
HY_MAX_DECAY = math.log(HY_TARGET) / HY_FAST_DECAY
EPS = 1e-6
DFT_P = 256
HY_FWD_ROWS = 1024
HY_INV_ROWS = 512
HY_TILE = 512
HY_LANE_GROUP = 256
HIGHEST = lax.Precision.HIGHEST

COL_GATES = 0
COL_ATT_Q = 24
COL_ATT_K = 32
COL_ATT_V = 34
COL_HG_Q = 36
COL_HG_F = 44
COL_HG_B = 52
COL_HG_I = 60
COL_HG_G = 68
COL_HY = 76
N_COLS = 100


def _pick(n, candidates):
    for c in candidates:
        if n % c == 0:
            return c
    raise ValueError(f"no tile in {candidates} divides {n}")


def _nt_dot(a, b):
    return lax.dot_general(a, b, (((1,), (1,)), ((), ())), preferred_element_type=F32)


def _adaln_kernel(c_ref, w_ref, b_ref, o_ref):
    cv = c_ref[...]
    s = cv * jax.nn.sigmoid(cv)
    o_ref[0] = jnp.dot(s, w_ref[0], precision=HIGHEST, preferred_element_type=F32) + b_ref[0]


def _adaln(cvec, ada_w, ada_b):
    depth, d, n6 = ada_w.shape
    rows = cvec.shape[0]
    tn = _pick(n6, (1536, 1024, 512, 128))
    return pl.pallas_call(
        _adaln_kernel,
        grid=(depth, n6 // tn),
        in_specs=[pl.BlockSpec((rows, d), lambda l, j: (0, 0)),
                  pl.BlockSpec((1, d, tn), lambda l, j: (l, 0, j)),
                  pl.BlockSpec((1, 1, tn), lambda l, j: (l, 0, j))],
        out_specs=pl.BlockSpec((1, rows, tn), lambda l, j: (l, 0, j)),
        out_shape=jax.ShapeDtypeStruct((depth, rows, n6), F32),
        name="adaln",
    )(cvec, ada_w, ada_b.reshape(depth, 1, n6))


def _modulated_norm(x, g, sh_l, sc_l, sh_c, sc_c, row0, n_lat):
    y = x * lax.rsqrt(jnp.mean(x * x, axis=-1, keepdims=True) + EPS) * g
    row = row0 + lax.broadcasted_iota(jnp.int32, (x.shape[0], 1), 0)
    is_ctx = row >= n_lat
    sc = jnp.where(is_ctx, sc_c, sc_l)
    sh = jnp.where(is_ctx, sh_c, sh_l)
    return y * (1.0 + sc) + sh


def _mod_specs(layer, ctx_row, k_shift, k_scale, d, nargs):
    def spec(row_fn, k):
        if nargs == 3:
            return pl.BlockSpec((1, 1, 1, d), lambda b, t, j: (layer, row_fn(b), 0, k))
        return pl.BlockSpec((1, 1, 1, d), lambda b, t: (layer, row_fn(b), 0, k))
    lat = lambda b: b
    ctx = lambda b: ctx_row
    return [spec(lat, k_shift), spec(lat, k_scale), spec(ctx, k_shift), spec(ctx, k_scale)]


def _inproj_kernel(x_ref, shl_ref, scl_ref, shc_ref, scc_ref, g_ref, w_ref, o_ref, h_ref, *, n_lat, tm):
    t = pl.program_id(1)

    @pl.when(pl.program_id(2) == 0)
    def _():
        h = _modulated_norm(x_ref[0], g_ref[...], shl_ref[0, 0], scl_ref[0, 0], shc_ref[0, 0], scc_ref[0, 0],
                            t * tm, n_lat)
        h_ref[...] = h.astype(BF16)

    o_ref[0] = jnp.dot(h_ref[...], w_ref[...], preferred_element_type=F32).astype(BF16)


def _inproj(x, mod, layer, ctx_row, g, w_bf, n_lat):
    bsz, seq, d = x.shape
    n = w_bf.shape[1]
    tm = _pick(seq, (768, 512, 256))
    tn = _pick(n, (3200, 2560, 1280, 640, 128))
    return pl.pallas_call(
        functools.partial(_inproj_kernel, n_lat=n_lat, tm=tm),
        grid=(bsz, seq // tm, n // tn),
        in_specs=[pl.BlockSpec((1, tm, d), lambda b, t, j: (b, t, 0))]
        + _mod_specs(layer, ctx_row, 0, 1, d, 3)
        + [pl.BlockSpec((1, d), lambda b, t, j: (0, 0)),
           pl.BlockSpec((d, tn), lambda b, t, j: (0, j))],
        out_specs=pl.BlockSpec((1, tm, tn), lambda b, t, j: (b, t, j)),
        out_shape=jax.ShapeDtypeStruct((bsz, seq, n), BF16),
        scratch_shapes=[pltpu.VMEM((tm, d), BF16)],
        compiler_params=pltpu.CompilerParams(dimension_semantics=("parallel", "parallel", "arbitrary")),
        name="inproj",
    )(x, mod, mod, mod, mod, g.reshape(1, d), w_bf)


def _rope(x, cos, sin_signed):
    lane = lax.broadcasted_iota(jnp.int32, x.shape, 1)
    first = (lane % 64) < 32
    rx = jnp.where(first, pltpu.roll(x, 96, 1), pltpu.roll(x, 32, 1))
    return x * cos + rx * sin_signed


def _head_norm(x, g):
    return x * lax.rsqrt(jnp.mean(x * x, axis=-1, keepdims=True) + EPS) * g


def _attn_kernel(q_ref, k_ref, v_ref, cq_ref, sq_ref, ck_ref, sk_ref, gq_ref, gk_ref, o_ref, ks_ref, vs_ref,
                 *, n_lat, tq):
    qi = pl.program_id(2)

    @pl.when(qi == 0)
    def _():
        k = _head_norm(k_ref[0].astype(F32), gk_ref[...])
        ks_ref[...] = _rope(k, ck_ref[...], sk_ref[...]).astype(BF16)
        vs_ref[:, :HEAD_DIM] = v_ref[0]
        vs_ref[:, HEAD_DIM:] = jnp.ones((vs_ref.shape[0], HEAD_DIM), BF16)

    scale = HEAD_DIM ** -0.5 * math.log2(math.e)

    def attend(keys, vals):
        qall = q_ref[0].astype(F32)
        qs = []
        for g in range(ATT_GROUP):
            qg = _head_norm(qall[:, g * HEAD_DIM:(g + 1) * HEAD_DIM], gq_ref[...])
            qs.append((_rope(qg, cq_ref[...], sq_ref[...]) * scale).astype(BF16))
        ss = [_nt_dot(q, keys) for q in qs]
        ps = [jnp.exp2(s - jnp.max(s, axis=-1, keepdims=True)).astype(BF16) for s in ss]
        for g, p in enumerate(ps):
            o2 = jnp.dot(p, vals, preferred_element_type=F32)
            o = o2[:, :HEAD_DIM] / o2[:, HEAD_DIM:]
            o_ref[0, :, g * HEAD_DIM:(g + 1) * HEAD_DIM] = o.astype(BF16)

    is_lat = qi * tq < n_lat

    @pl.when(is_lat)
    def _():
        attend(ks_ref[...], vs_ref[...])

    @pl.when(jnp.logical_not(is_lat))
    def _():
        attend(ks_ref[n_lat:, :], vs_ref[n_lat:, :])


def _attention(z, cos, sin_signed, gq, gk, n_lat, need_ctx):
    bsz, seq, _ = z.shape
    tq = 256
    n_q = (seq if need_ctx else n_lat) // tq
    gw = ATT_GROUP * HEAD_DIM
    return pl.pallas_call(
        functools.partial(_attn_kernel, n_lat=n_lat, tq=tq),
        grid=(bsz, ATT_KV_HEADS, n_q),
        in_specs=[pl.BlockSpec((1, tq, gw), lambda b, h, i: (b, i, COL_ATT_Q // ATT_GROUP + h)),
                  pl.BlockSpec((1, seq, HEAD_DIM), lambda b, h, i: (b, 0, COL_ATT_K + h)),
                  pl.BlockSpec((1, seq, HEAD_DIM), lambda b, h, i: (b, 0, COL_ATT_V + h)),
                  pl.BlockSpec((tq, HEAD_DIM), lambda b, h, i: (i, 0)),
                  pl.BlockSpec((tq, HEAD_DIM), lambda b, h, i: (i, 0)),
                  pl.BlockSpec((seq, HEAD_DIM), lambda b, h, i: (0, 0)),
                  pl.BlockSpec((seq, HEAD_DIM), lambda b, h, i: (0, 0)),
                  pl.BlockSpec((1, HEAD_DIM), lambda b, h, i: (0, 0)),
                  pl.BlockSpec((1, HEAD_DIM), lambda b, h, i: (0, 0))],
        out_specs=pl.BlockSpec((1, tq, gw), lambda b, h, i: (b, i, h)),
        out_shape=jax.ShapeDtypeStruct((bsz, n_q * tq, ATT_HEADS * HEAD_DIM), BF16),
        scratch_shapes=[pltpu.VMEM((seq, HEAD_DIM), BF16), pltpu.VMEM((seq, 2 * HEAD_DIM), BF16)],
        compiler_params=pltpu.CompilerParams(dimension_semantics=("parallel", "parallel", "arbitrary")),
        name="attention",
    )(z, z, z, cos, sin_signed, cos, sin_signed, gq.reshape(1, HEAD_DIM), gk.reshape(1, HEAD_DIM))


def _rope_tables(n_lat, n_ctx):
    half = HEAD_DIM // 2
    inv = ROPE_THETA ** (-np.arange(0, half, 2, dtype=np.float64) / half)
    t = np.arange(n_lat)
    ar = (t // GRID_W)[:, None] * inv
    ac = (t % GRID_W)[:, None] * inv
    ang = np.concatenate([ar, ar, ac, ac], axis=-1)
    cos = np.concatenate([np.cos(ang), np.ones((n_ctx, HEAD_DIM))], axis=0)
    sin = np.concatenate([np.sin(ang), np.zeros((n_ctx, HEAD_DIM))], axis=0)
    sign = np.where((np.arange(HEAD_DIM) % 64) < 32, -1.0, 1.0)
    return jnp.asarray(cos, F32), jnp.asarray(sin * sign, F32)


def _hgrn_masks():
    c = HG_CHUNK
    idx = np.arange(c)
    t, s = idx[:, None], idx[None, :]

    def halving(size):
        half = size // 2
        return (t // size == s // size) & (t % size >= half) & (s % size < half)

    diag = (t // 8 == s // 8) & (s <= t)
    flip = lambda m: m[::-1, ::-1]
    fwd = np.stack([np.concatenate([halving(64), halving(32)], axis=1),
                    np.concatenate([halving(16), diag], axis=1)])
    bwd = np.stack([np.concatenate([flip(halving(64)), flip(halving(32))], axis=1),
                    np.concatenate([flip(halving(16)), flip(diag)], axis=1)])
    return jnp.asarray(np.stack([fwd, bwd]), F32)


def _hgrn_exponents(zg, lb, reverse):
    tile = 8
    nt = HG_CHUNK // tile
    sig = jax.nn.sigmoid(zg.astype(F32))
    f = sig if lb is None else lb + (1.0 - lb) * sig
    lf = jnp.log2(jnp.maximum(f, HG_MIN_F))
    kk = 1.0 - f
    p = lf.reshape(nt, tile, LANES)
    sub = lax.broadcasted_iota(jnp.int32, p.shape, 1)
    for k in (1, 2, 4):
        if reverse:
            p = p + jnp.where(sub < tile - k, pltpu.roll(p, tile - k, 1), 0.0)
        else:
            p = p + jnp.where(sub >= k, pltpu.roll(p, k, 1), 0.0)
    order = list(range(nt - 1, -1, -1)) if reverse else list(range(nt))
    tot_row, mid_row = (0, 4) if reverse else (tile - 1, 3)
    s = [p[j] for j in order]
    tb = [jnp.broadcast_to(p[j, tot_row:tot_row + 1, :], (tile, LANES)) for j in order]
    mb = [jnp.broadcast_to(p[j, mid_row:mid_row + 1, :], (tile, LANES)) for j in order]
    t01, t23, t45 = tb[0] + tb[1], tb[2] + tb[3], tb[4] + tb[5]
    t123, t456 = tb[1] + t23, t45 + tb[6]
    c4 = t01 + t23
    before = [None, tb[0], t01, t01 + tb[2], c4, c4 + tb[4], c4 + t45, c4 + t456]
    total = before[7] + tb[7]
    cum = [s[0]] + [before[k] + s[k] for k in range(1, nt)]
    suf = [total - cum[k] for k in range(nt)]
    w8 = [s[k] - mb[k] for k in range(nt)]
    w16 = [s[k] if k % 2 else tb[k] - s[k] for k in range(nt)]
    w32 = [t01 - s[0], tb[1] - s[1], s[2], tb[2] + s[3], t45 - s[4], tb[5] - s[5], s[6], tb[6] + s[7]]
    w64 = [c4 - s[0], t123 - s[1], t23 - s[2], tb[3] - s[3], s[4], tb[4] + s[5], t45 + s[6], t456 + s[7]]

    def natural(parts):
        return jnp.concatenate([parts[order.index(j)] for j in range(nt)], axis=0)

    ex = {"cum": natural(cum), "suf": natural(suf), 64: natural(w64), 32: natural(w32), 16: natural(w16),
          8: natural(w8), "tot": total[0:1]}
    return kk, ex


def _hgrn_scores(q, kk, ex, st):
    qf = q.astype(F32)
    inter = _nt_dot((qf * jnp.exp2(ex["cum"])).astype(BF16), st.astype(BF16))
    zeros = jnp.zeros(q.shape, BF16)
    pairs = []
    for la, lb_ in ((64, 32), (16, 8)):
        ea, eb = jnp.exp2(ex[la]), jnp.exp2(ex[lb_])
        eb_k = jnp.exp2(-ex[lb_]) if lb_ == 8 else eb
        qa = jnp.concatenate([(qf * ea).astype(BF16), (qf * eb).astype(BF16)], axis=1)
        ka = jnp.concatenate([(kk * ea).astype(BF16), zeros], axis=1)
        kb = jnp.concatenate([zeros, (kk * eb_k).astype(BF16)], axis=1)
        pairs.append(_nt_dot(qa, jnp.concatenate([ka, kb], axis=0)))
    return inter, pairs


def _hgrn_output(inter, pairs, v, mask):
    p0 = pairs[0] * mask[0]
    p1 = jnp.where(mask[1] > 0.0, pairs[1], 0.0)
    probs = jnp.concatenate([p0.astype(BF16), p1.astype(BF16)], axis=1)
    return inter + jnp.dot(probs, jnp.concatenate([v, v, v, v], axis=0), preferred_element_type=F32)


def _hgrn_state(kk, ex, v, st):
    k_suf = (kk * jnp.exp2(ex["suf"])).astype(BF16)
    upd = lax.dot_general(v, k_suf, (((0,), (0,)), ((), ())), preferred_element_type=F32)
    return st * jnp.exp2(ex["tot"]) + upd


def _hgrn_kernel(q_ref, zf_ref, zb_ref, i_ref, g_ref, lbraw_ref, gn_ref, mask_ref, o_ref,
                 acc_ref, st_ref, *, layer, n_lat, n_ctx, need_ctx):
    c = HG_CHUNK
    nc_lat, nc_ctx = n_lat // c, n_ctx // c
    nc = nc_lat + nc_ctx
    raw = lbraw_ref[...].astype(F32)
    ew = jnp.exp(raw - jnp.max(raw, axis=0, keepdims=True))
    sm = ew / jnp.sum(ew, axis=0, keepdims=True)
    lower = jnp.zeros_like(sm[0])
    for dpt in range(1, layer + 1):
        lower = lower + sm[dpt]

    acc_ref[...] = jnp.zeros_like(acc_ref)
    st_ref[...] = jnp.zeros_like(st_ref)

    def step(n, carry, with_outputs=True):
        cf = jnp.where(n < nc_ctx, nc_lat + n, n - nc_ctx)
        cb = nc - 1 - n
        chains = []
        for hh in range(HG_HEADS_PER_STEP):
            lanes = slice(hh * LANES, (hh + 1) * LANES)
            for d, (chunk, z_ref) in enumerate(((cf, zf_ref), (cb, zb_ref))):
                rows = pl.ds(pl.multiple_of(chunk * c, c), c)
                chains.append((hh, d, rows, lanes, z_ref))
        for g0 in range(0, len(chains), HG_CHAIN_GROUP):
            group = chains[g0:g0 + HG_CHAIN_GROUP]
            gates = [_hgrn_exponents(z_ref[0, rows, lanes], None if layer == 0 else lower[hh, d:d + 1], d == 1)
                     for hh, d, rows, lanes, z_ref in group]
            if with_outputs:
                scored = [_hgrn_scores(q_ref[0, rows, lanes], kk, ex, st_ref[hh, d])
                          for (hh, d, rows, lanes, _), (kk, ex) in zip(group, gates)]
                for (hh, d, rows, lanes, _), (inter, pairs) in zip(group, scored):
                    out = _hgrn_output(inter, pairs, i_ref[0, rows, lanes], mask_ref[d])
                    acc_ref[rows, lanes] = acc_ref[rows, lanes] + out
            for (hh, d, rows, lanes, _), (kk, ex) in zip(group, gates):
                st_ref[hh, d] = _hgrn_state(kk, ex, i_ref[0, rows, lanes], st_ref[hh, d])
        return carry

    if need_ctx:
        lax.fori_loop(0, nc, step, 0)
    else:
        lax.fori_loop(0, nc_ctx, functools.partial(step, with_outputs=False), 0)
        lax.fori_loop(nc_ctx, nc, step, 0)

    rows_out = o_ref.shape[1]
    for hh in range(HG_HEADS_PER_STEP):
        lanes = slice(hh * LANES, (hh + 1) * LANES)
        o = acc_ref[:, lanes]
        y = o * lax.rsqrt(jnp.mean(o * o, axis=-1, keepdims=True) + EPS) * gn_ref[...]
        g = g_ref[0, :rows_out, lanes].astype(F32)
        o_ref[0, :, lanes] = (y * (g * jax.nn.sigmoid(g))).astype(BF16)


def _hgrn(z, lb_raw, g_norm, layer, n_lat, n_ctx, need_ctx):
    bsz, seq, _ = z.shape
    depth = lb_raw.shape[0]
    hps = HG_HEADS_PER_STEP
    width = hps * LANES
    masks = _hgrn_masks()
    rows_out = seq if need_ctx else n_lat
    lbr = lb_raw.reshape(depth, 2, HG_HEADS, LANES).transpose(0, 2, 1, 3)
    col = lambda c0: pl.BlockSpec((1, seq, width), lambda b, h: (b, 0, c0 // hps + h))
    return pl.pallas_call(
        functools.partial(_hgrn_kernel, layer=layer, n_lat=n_lat, n_ctx=n_ctx, need_ctx=need_ctx),
        grid=(bsz, HG_HEADS // hps),
        in_specs=[col(COL_HG_Q), col(COL_HG_F), col(COL_HG_B), col(COL_HG_I), col(COL_HG_G),
                  pl.BlockSpec((depth, hps, 2, LANES), lambda b, h: (0, h, 0, 0)),
                  pl.BlockSpec((1, LANES), lambda b, h: (0, 0)),
                  pl.BlockSpec((2, 2, HG_CHUNK, LANES), lambda b, h: (0, 0, 0, 0))],
        out_specs=pl.BlockSpec((1, rows_out, width), lambda b, h: (b, 0, h)),
        out_shape=jax.ShapeDtypeStruct((bsz, rows_out, HG_HEADS * LANES), BF16),
        scratch_shapes=[pltpu.VMEM((rows_out, width), F32), pltpu.VMEM((hps, 2, LANES, LANES), F32)],
        compiler_params=pltpu.CompilerParams(dimension_semantics=("parallel", "parallel")),
        name="hgrn",
    )(z, z, z, z, z, lbr, g_norm.reshape(1, LANES), masks)


def _dft_matrices(n):
    size = 2 * n
    k = np.arange(n)[:, None]
    t = np.arange(n)[None, :]
    ang = (2.0 * np.pi / size) * ((k * t) % size)
    cos, sin = np.cos(ang), np.sin(ang)
    sin[0, :] = np.where(np.arange(n) % 2 == 0, 1.0, -1.0)
    fwd = np.concatenate([cos.reshape(n // DFT_P, DFT_P, n), sin.reshape(n // DFT_P, DFT_P, n)], axis=1)
    fwd = fwd.reshape(size, n)
    pairs = [(cos[i:i + DFT_P], sin[i:i + DFT_P]) for i in range(0, n, DFT_P)]
    filt = _group_filter_rows(pairs, _filter_group(n))
    return jnp.asarray(filt, BF16), (jnp.asarray(fwd, BF16), jnp.asarray(fwd.T.copy(), BF16))


def _hyena_positions(n):
    t = np.linspace(0.0, 1.0, n)[:, None]
    w = (2.0 * math.pi / n) * np.arange(n)[:, None]
    f = np.linspace(1e-4, HY_BANDS - 1, HY_BANDS)[None, :]
    z = np.concatenate([t, np.cos(f * w), -np.sin(f * w)], axis=-1)
    return jnp.asarray(np.pad(z, ((0, 0), (0, LANES - HY_EMB))), F32)


def _filter_kernel(pos_ref, w1_ref, b1_ref, w2_ref, b2_ref, fr_ref, w3f_ref, w3b_ref, dl_ref, fs_ref, fd_ref,
                   h_ref):
    pos = pos_ref[...]

    @pl.when((pl.program_id(0) == 0) & (pl.program_id(1) == 0))
    def _():
        h1 = jnp.sin(fr_ref[0:1] * (jnp.dot(pos, w1_ref[...], precision=HIGHEST, preferred_element_type=F32)
                                    + b1_ref[...]))
        h_ref[...] = jnp.sin(fr_ref[1:2] * (jnp.dot(h1, w2_ref[...], precision=HIGHEST,
                                                    preferred_element_type=F32) + b2_ref[...]))

    h = h_ref[...]
    window = jnp.exp(-pos[:, 0:1] * dl_ref[...])
    hf = jnp.dot(h, w3f_ref[...], precision=HIGHEST, preferred_element_type=F32) * window
    hb = jnp.dot(h, w3b_ref[...], precision=HIGHEST, preferred_element_type=F32) * window
    row = lax.broadcasted_iota(jnp.int32, (pos.shape[0], 1), 0)
    hb = jnp.where(row == 0, 0.0, hb)
    norm = jnp.sum(jnp.abs(hf), axis=0, keepdims=True) + jnp.sum(jnp.abs(hb), axis=0, keepdims=True) + EPS
    inv = 1.0 / norm
    fs_ref[0] = ((hf + hb) * inv).astype(BF16)
    fd_ref[0] = ((hf - hb) * inv).astype(BF16)


def _filter_group(n):
    chunks = n // DFT_P
    return 4 if chunks % 4 == 0 else 1


def _group_filter_rows(pairs, group):
    out = []
    for g0 in range(0, len(pairs), group):
        out += [c for c, _ in pairs[g0:g0 + group]] + [s for _, s in pairs[g0:g0 + group]]
    return np.concatenate(out)


def _filter_dft_kernel(a_ref, fs_ref, fd_ref, o_ref, *, n, group):
    m = pl.program_id(2)
    p = DFT_P
    gp = group * p
    first = (lax.broadcasted_iota(jnp.int32, (gp, 1), 0) == 0) & (m == 0)
    weight = jnp.where(first, 1.0, 2.0) / (2 * n)
    hc = jnp.dot(a_ref[:gp], fs_ref[0], preferred_element_type=F32) * weight
    hs = jnp.dot(a_ref[gp:], fd_ref[0], preferred_element_type=F32) * weight
    for g in range(group):
        o_ref[0, 2 * p * g:2 * p * g + p] = hc[g * p:(g + 1) * p]
        o_ref[0, 2 * p * g + p:2 * p * (g + 1)] = hs[g * p:(g + 1) * p]

    @pl.when(m == 0)
    def _():
        hny = jnp.dot(a_ref[gp:gp + 16], fs_ref[0], preferred_element_type=F32)[:8]
        o_ref[0, p:p + 8] = jnp.where(first[:8], hny / (2 * n), hs[:8])


def _hyena_filters(n, fwd, w1p, b1, w2, b2, w3, freq):
    hid = w2.shape[0]
    c = w3.shape[1] // 4
    tn = _pick(c, (512, 256, 128))
    nct = c // tn
    pos = _hyena_positions(n)
    deltas = jnp.asarray(np.abs(np.linspace(HY_MIN_DECAY, HY_MAX_DECAY, c))[None, :], F32)
    full = lambda shape: pl.BlockSpec(shape, lambda o, j: (0,) * len(shape))
    fsum, fdiff = pl.pallas_call(
        _filter_kernel,
        grid=(2, nct),
        in_specs=[full((n, LANES)), full((LANES, hid)), full((1, hid)), full((hid, hid)), full((1, hid)),
                  full((2, hid)),
                  pl.BlockSpec((hid, tn), lambda o, j: (0, (2 * o) * nct + j)),
                  pl.BlockSpec((hid, tn), lambda o, j: (0, (2 * o + 1) * nct + j)),
                  pl.BlockSpec((1, tn), lambda o, j: (0, j))],
        out_specs=[pl.BlockSpec((1, n, tn), lambda o, j: (o, 0, j))] * 2,
        out_shape=[jax.ShapeDtypeStruct((2, n, c), BF16)] * 2,
        scratch_shapes=[pltpu.VMEM((n, hid), F32)],
        compiler_params=pltpu.CompilerParams(dimension_semantics=("arbitrary", "arbitrary")),
        name="hyena_filter",
    )(pos, w1p, b1.reshape(1, hid), w2, b2.reshape(1, hid), freq, w3, w3, deltas)
    group = _filter_group(n)
    rows = 2 * DFT_P * group
    return pl.pallas_call(
        functools.partial(_filter_dft_kernel, n=n, group=group),
        grid=(2, nct, 2 * n // rows),
        in_specs=[pl.BlockSpec((rows, n), lambda o, j, m: (m, 0)),
                  pl.BlockSpec((1, n, tn), lambda o, j, m: (o, 0, j)),
                  pl.BlockSpec((1, n, tn), lambda o, j, m: (o, 0, j))],
        out_specs=pl.BlockSpec((1, rows, tn), lambda o, j, m: (o, m, j)),
        out_shape=jax.ShapeDtypeStruct((2, 2 * n, c), F32),
        name="hyena_filter_dft",
    )(fwd, fsum, fdiff)


def _short_conv(u, w, b):
    n = u.shape[0]
    row = lax.broadcasted_iota(jnp.int32, (n, 1), 0)
    prev = jnp.where(row == 0, 0.0, pltpu.roll(u, 1, 0))
    nxt = jnp.where(row == n - 1, 0.0, pltpu.roll(u, n - 1, 0))
    return prev * w[0:1] + u * w[1:2] + nxt * w[2:3] + b


def _shortconv_kernel(u_ref, w_ref, b_ref, o_ref):
    o_ref[0] = _short_conv(u_ref[0].astype(F32), w_ref[...], b_ref[...]).astype(BF16)


def _shortconv(z, w, b, n, row_blk):
    bsz = z.shape[0]
    width = w.shape[1]
    tn = HY_TILE
    assert (COL_HY * LANES) % tn == 0 and width % tn == 0
    c0 = COL_HY * LANES // tn
    return pl.pallas_call(
        _shortconv_kernel,
        grid=(bsz, width // tn),
        in_specs=[pl.BlockSpec((1, n, tn), lambda bb, j: (bb, row_blk, c0 + j)),
                  pl.BlockSpec((3, tn), lambda bb, j: (0, j)),
                  pl.BlockSpec((1, tn), lambda bb, j: (0, j))],
        out_specs=pl.BlockSpec((1, n, tn), lambda bb, j: (bb, 0, j)),
        out_shape=jax.ShapeDtypeStruct((bsz, n, width), BF16),
        compiler_params=pltpu.CompilerParams(dimension_semantics=("parallel", "parallel")),
        name="hyena_shortconv",
    )(z, w, b.reshape(1, width))


def _conv_fwd_kernel(a_ref, v_ref, h_ref, y_ref, *, chunks):
    p = DFT_P
    zf = jnp.dot(a_ref[...], v_ref[0], preferred_element_type=F32)
    for k in range(chunks):
        lo = 2 * p * k
        zc, zs = zf[lo:lo + p], zf[lo + p:lo + 2 * p]
        hc, hs = h_ref[0, lo:lo + p], h_ref[0, lo + p:lo + 2 * p]
        yc, ys = zc * hc - zs * hs, zc * hs + zs * hc
        if k == 0:
            real_row = (lax.broadcasted_iota(jnp.int32, (p, 1), 0) == 0) & (pl.program_id(1) == 0)
            yc = jnp.where(real_row, zc * hc, yc)
            ys = jnp.where(real_row, zs * hs, ys)
        y_ref[0, lo:lo + p] = yc.astype(BF16)
        y_ref[0, lo + p:lo + 2 * p] = ys.astype(BF16)


def _conv_fwd(fwd, v_arr, v_col, spec, order, n, row_blk):
    bsz = v_arr.shape[0]
    c = spec.shape[2]
    rows = min(HY_FWD_ROWS, 2 * n)
    return pl.pallas_call(
        functools.partial(_conv_fwd_kernel, chunks=rows // (2 * DFT_P)),
        grid=(bsz, 2 * n // rows),
        in_specs=[pl.BlockSpec((rows, n), lambda b, m: (m, 0)),
                  pl.BlockSpec((1, n, c), lambda b, m: (b, row_blk, v_col)),
                  pl.BlockSpec((1, rows, c), lambda b, m: (order, m, 0))],
        out_specs=pl.BlockSpec((1, rows, c), lambda b, m: (b, m, 0)),
        out_shape=jax.ShapeDtypeStruct((bsz, 2 * n, c), BF16),
        compiler_params=pltpu.CompilerParams(dimension_semantics=("parallel", "arbitrary")),
        name="hyena_conv_fwd",
    )(fwd, v_arr, spec)


def _conv_inv_kernel(g_ref, y_ref, gate_ref, zin_ref, bias_ref, o_ref):
    y = jnp.dot(g_ref[...], y_ref[0], preferred_element_type=F32)
    o_ref[0] = (gate_ref[0].astype(F32) * (y + zin_ref[0].astype(F32) * bias_ref[0])).astype(BF16)


def _conv_inv(inv, y, gate, zin, bias, order, n):
    bsz, _, c = y.shape
    tm = min(HY_INV_ROWS, n)

    def rows_of(triple):
        _, row0, col = triple
        return pl.BlockSpec((1, tm, c), lambda b, m: (b, row0 // tm + m, col))

    return pl.pallas_call(
        _conv_inv_kernel,
        grid=(bsz, n // tm),
        in_specs=[pl.BlockSpec((tm, 2 * n), lambda b, m: (m, 0)),
                  pl.BlockSpec((1, 2 * n, c), lambda b, m: (b, 0, 0)),
                  rows_of(gate), rows_of(zin),
                  pl.BlockSpec((1, 1, c), lambda b, m: (order, 0, 0))],
        out_specs=pl.BlockSpec((1, tm, c), lambda b, m: (b, m, 0)),
        out_shape=jax.ShapeDtypeStruct((bsz, n, c), BF16),
        compiler_params=pltpu.CompilerParams(dimension_semantics=("parallel", "arbitrary")),
        name="hyena_conv_inv",
    )(inv, y, gate[0], zin[0], bias.reshape(2, 1, c))


def _hyena_dense(z, sw, sb, spectrum, dft, bias, n, row_blk):
    u = _shortconv(z, sw, sb, n, row_blk)
    x1_col, x2_col, v_col = 0, 1, 2
    fwd, inv = dft
    y = _conv_fwd(fwd, u, v_col, spectrum, 0, n, 0)
    z1 = _conv_inv(inv, y, (u, 0, x1_col), (u, 0, v_col), bias, 0, n)
    y = _conv_fwd(fwd, z1, 0, spectrum, 1, n, 0)
    return _conv_inv(inv, y, (u, 0, x2_col), (z1, 0, 0), bias, 1, n)


def _radix8_frequencies(n):
    p = n // 8
    j = np.arange(p)
    return [8 * j, 4 + 8 * j, 1 + 8 * j, 1 + 8 * (j + p), 2 + 8 * j, 2 + 8 * (j + p), 3 + 8 * j, 3 + 8 * (j + p)]


def _radix8_matrices(n):
    size, p = 2 * n, n // 8
    freqs = _radix8_frequencies(n)

    def cos_sin(k, length):
        ang = (2.0 * np.pi / size) * ((k[:, None] * np.arange(length)[None, :]) % size)
        return np.cos(ang), np.sin(ang)

    def nyquist(sin_rows, length):
        sin_rows[0, :] = np.where(np.arange(length) % 2 == 0, 1.0, -1.0)
        return sin_rows

    pairs = []
    for idx, k in enumerate(freqs):
        c, s = cos_sin(k, n)
        pairs.append((c, nyquist(s, n) if idx == 0 else s))
    filt = _group_filter_rows(pairs, _filter_group(n))
    slab = size // 8
    c0, s0 = cos_sin(freqs[0], slab)
    c4, s4 = cos_sin(freqs[1], slab)
    mats = [np.concatenate([c0, nyquist(s0, slab)]), np.concatenate([c4, s4])]
    for k1 in (1, 2, 3):
        ca, sa = cos_sin(freqs[2 * k1], slab)
        cb, sb = cos_sin(freqs[2 * k1 + 1], slab)
        mats.append(np.block([[ca, sa], [sa, -ca], [cb, sb], [sb, -cb]]))
    to_bf = lambda a: jnp.asarray(a, BF16)
    return to_bf(filt), [to_bf(m) for m in mats], [to_bf(m.T.copy()) for m in mats]


def _conv_fwd8_kernel(*refs, pre_conv):
    if pre_conv:
        v_ref, w_ref, b_ref, r0_ref, r4_ref, r1_ref, r2_ref, r3_ref, h_ref, y_ref, vs_ref = refs
    else:
        v_ref, r0_ref, r4_ref, r1_ref, r2_ref, r3_ref, h_ref, y_ref = refs
    n, tn = v_ref.shape[1], v_ref.shape[2]
    slab, p = n // 4, n // 8
    bf = lambda x: x.astype(BF16)
    cat = lambda re, im: jnp.concatenate([bf(re), bf(im)], axis=0)
    groups = [slice(g * HY_LANE_GROUP, (g + 1) * HY_LANE_GROUP) for g in range(tn // HY_LANE_GROUP)]
    inputs = []
    for lanes in groups:
        v = v_ref[0, :, lanes].astype(F32)
        if pre_conv:
            v = _short_conv(v, w_ref[:, lanes], b_ref[:, lanes])
            vs_ref[0, :, lanes] = v.astype(BF16)
        z0, z1, z2, z3 = (v[i * slab:(i + 1) * slab] for i in range(4))
        e, o = z0 + z2, z1 + z3
        a, b = (z1 - z3) * (0.5 ** 0.5), o * (0.5 ** 0.5)
        inputs.append((bf(e + o), bf(e - o), cat(z0 + a, -z2 - b), cat(z0 - z2, z3 - z1), cat(z0 - a, z2 - b)))
    spectra = [[jnp.dot(r[...], x, preferred_element_type=F32)
                for r, x in zip((r0_ref, r4_ref, r1_ref, r2_ref, r3_ref), xs)] for xs in inputs]
    for lanes, spec in zip(groups, spectra):
        chunks = [spec[0], spec[1]] + [s[i * 2 * p:(i + 1) * 2 * p] for s in spec[2:] for i in range(2)]
        for c, zf in enumerate(chunks):
            lo = 2 * p * c
            zc, zs = zf[:p], zf[p:]
            hc, hs = h_ref[0, lo:lo + p, lanes], h_ref[0, lo + p:lo + 2 * p, lanes]
            yc, ys = zc * hc - zs * hs, zc * hs + zs * hc
            if c == 0:
                real_row = lax.broadcasted_iota(jnp.int32, (p, 1), 0) == 0
                yc = jnp.where(real_row, zc * hc, yc)
                ys = jnp.where(real_row, zs * hs, ys)
            y_ref[0, lo:lo + p, lanes] = yc.astype(BF16)
            y_ref[0, lo + p:lo + 2 * p, lanes] = ys.astype(BF16)


def _const_spec(shape):
    return pl.BlockSpec(shape, lambda j, b: (0,) * len(shape), pipeline_mode=pl.Buffered(1))


def _conv_fwd8(mats, v_arr, v_col, spec, order, n, short=None):
    bsz = v_arr.shape[0]
    c = spec.shape[2]
    tn = HY_TILE
    in_specs = [pl.BlockSpec((1, n, tn), lambda j, b: (b, 0, v_col + j))]
    args = [v_arr]
    out_specs = [pl.BlockSpec((1, 2 * n, tn), lambda j, b: (b, 0, j))]
    out_shape = [jax.ShapeDtypeStruct((bsz, 2 * n, c), BF16)]
    if short is not None:
        w, bvec, wcol = short
        in_specs += [pl.BlockSpec((3, tn), lambda j, b: (0, wcol + j)),
                     pl.BlockSpec((1, tn), lambda j, b: (0, wcol + j))]
        args += [w, bvec.reshape(1, -1)]
        out_specs.append(pl.BlockSpec((1, n, tn), lambda j, b: (b, 0, j)))
        out_shape.append(jax.ShapeDtypeStruct((bsz, n, c), BF16))
    in_specs += [_const_spec(m.shape) for m in mats]
    in_specs.append(pl.BlockSpec((1, 2 * n, tn), lambda j, b: (order, 0, j)))
    res = pl.pallas_call(
        functools.partial(_conv_fwd8_kernel, pre_conv=short is not None),
        grid=(c // tn, bsz),
        in_specs=in_specs,
        out_specs=out_specs,
        out_shape=out_shape,
        compiler_params=pltpu.CompilerParams(dimension_semantics=("parallel", "parallel")),
        name="hyena_conv_fwd8",
    )(*args, *mats, spec)
    return res if short is not None else res[0]


def _conv_inv8_kernel(y_ref, r0_ref, r4_ref, r1_ref, r2_ref, r3_ref, gate_ref, w_ref, b_ref, zin_ref, bias_ref,
                      o_ref):
    n, tn = o_ref.shape[1], o_ref.shape[2]
    slab = n // 4
    groups = [slice(g * HY_LANE_GROUP, (g + 1) * HY_LANE_GROUP) for g in range(tn // HY_LANE_GROUP)]
    bounds = (0, slab, 2 * slab, 4 * slab, 6 * slab, 8 * slab)
    gates = [_short_conv(gate_ref[0, :, lanes].astype(F32), w_ref[:, lanes], b_ref[:, lanes]) for lanes in groups]
    skips = [zin_ref[0, :, lanes].astype(F32) * bias_ref[0, :, lanes] for lanes in groups]
    parts = [[jnp.dot(r[...], y_ref[0, lo:hi, lanes], preferred_element_type=F32)
              for r, lo, hi in zip((r0_ref, r4_ref, r1_ref, r2_ref, r3_ref), bounds[:-1], bounds[1:])]
             for lanes in groups]
    for lanes, gate, skip, (v0, v4, v1, v2, v3) in zip(groups, gates, skips, parts):
        (v1r, v1i), (v2r, v2i), (v3r, v3i) = ((v[:slab], v[slab:]) for v in (v1, v2, v3))
        s, d = v0 + v4, v0 - v4
        r = 0.5 ** 0.5
        outs = [s + v1r + v2r + v3r,
                d + (v1r - v1i - v3r - v3i) * r - v2i,
                s - v1i - v2r + v3i,
                d + (v3r - v3i - v1r - v1i) * r + v2i]
        for i, y in enumerate(outs):
            rows = slice(i * slab, (i + 1) * slab)
            o_ref[0, rows, lanes] = (gate[rows] * (y + skip[rows])).astype(BF16)


def _conv_inv8(mats_t, y, gate_arr, gate_col, short, zin, bias, order, n):
    bsz, _, c = y.shape
    tn = HY_TILE
    w, bvec, wcol = short
    return pl.pallas_call(
        _conv_inv8_kernel,
        grid=(c // tn, bsz),
        in_specs=[pl.BlockSpec((1, 2 * n, tn), lambda j, b: (b, 0, j))]
        + [_const_spec(m.shape) for m in mats_t]
        + [pl.BlockSpec((1, n, tn), lambda j, b: (b, 0, gate_col + j)),
           pl.BlockSpec((3, tn), lambda j, b: (0, wcol + j)),
           pl.BlockSpec((1, tn), lambda j, b: (0, wcol + j)),
           pl.BlockSpec((1, n, tn), lambda j, b: (b, 0, j)),
           pl.BlockSpec((1, 1, tn), lambda j, b: (order, 0, j))],
        out_specs=pl.BlockSpec((1, n, tn), lambda j, b: (b, 0, j)),
        out_shape=jax.ShapeDtypeStruct((bsz, n, c), BF16),
        compiler_params=pltpu.CompilerParams(dimension_semantics=("parallel", "parallel")),
        name="hyena_conv_inv8",
    )(y, *mats_t, gate_arr, w, bvec.reshape(1, -1), zin, bias.reshape(2, 1, c))


def _hyena_radix8(z, sw, sb, spectrum, mats, mats_t, bias, n):
    c = bias.shape[1]
    per = c // HY_TILE
    z_col = COL_HY * LANES // HY_TILE
    y, vs = _conv_fwd8(mats, z, z_col + 2 * per, spectrum, 0, n, short=(sw, sb, 2 * per))
    z1 = _conv_inv8(mats_t, y, z, z_col, (sw, sb, 0), vs, bias, 0, n)
    y = _conv_fwd8(mats, z1, 0, spectrum, 1, n)
    return _conv_inv8(mats_t, y, z, z_col + per, (sw, sb, per), z1, bias, 1, n)


def _merge_kernel(x_ref, att_ref, hg_ref, hy_ref, ga_ref, gh_ref, gy_ref, gl_ref, gc_ref, wb_ref, wo_ref,
                  o_ref, *, n_lat, tm):
    def branch(gate_ref, val_ref, k):
        gate = jax.nn.sigmoid(gate_ref[0].astype(F32))
        return gate * jnp.dot(val_ref[0], wb_ref[k], preferred_element_type=F32)

    merged = branch(ga_ref, att_ref, 0) + branch(gh_ref, hg_ref, 1) + branch(gy_ref, hy_ref, 2)
    r = jnp.dot(merged.astype(BF16), wo_ref[...], preferred_element_type=F32)
    row = pl.program_id(1) * tm + lax.broadcasted_iota(jnp.int32, (tm, 1), 0)
    gate = jnp.where(row >= n_lat, gc_ref[0, 0], gl_ref[0, 0])
    o_ref[0] = x_ref[0] + gate * r


def _merge(x, att, hg, hy, z, mod, layer, ctx_row, wb_bf, wo_bf, n_lat, rows):
    bsz, _, d = x.shape
    seq = rows
    tm = _pick(seq, (512, 384, 256))
    tok = lambda: pl.BlockSpec((1, tm, d), lambda b, t: (b, t, 0))
    gate = lambda k: pl.BlockSpec((1, tm, d), lambda b, t: (b, t, COL_GATES * LANES // d + k))
    return pl.pallas_call(
        functools.partial(_merge_kernel, n_lat=n_lat, tm=tm),
        grid=(bsz, seq // tm),
        in_specs=[tok(), tok(), tok(), tok(), gate(0), gate(1), gate(2),
                  pl.BlockSpec((1, 1, 1, d), lambda b, t: (layer, b, 0, 2)),
                  pl.BlockSpec((1, 1, 1, d), lambda b, t: (layer, ctx_row, 0, 2)),
                  pl.BlockSpec((3, d, d), lambda b, t: (0, 0, 0)),
                  pl.BlockSpec((d, d), lambda b, t: (0, 0))],
        out_specs=tok(),
        out_shape=jax.ShapeDtypeStruct((bsz, seq, d), F32),
        compiler_params=pltpu.CompilerParams(dimension_semantics=("parallel", "parallel")),
        name="merge",
    )(x, att, hg, hy, z, z, z, mod, mod, wb_bf, wo_bf)


def _mlp_kernel(x_ref, shl_ref, scl_ref, shc_ref, scc_ref, gl_ref, gc_ref, g_ref, w1_ref, w2_ref, o_ref,
                h_ref, acc_ref, *, n_lat, tm):
    t = pl.program_id(1)
    j = pl.program_id(2)

    @pl.when(j == 0)
    def _():
        h = _modulated_norm(x_ref[0], g_ref[...], shl_ref[0, 0], scl_ref[0, 0], shc_ref[0, 0], scc_ref[0, 0],
                            t * tm, n_lat)
        h_ref[...] = h.astype(BF16)
        acc_ref[...] = jnp.zeros_like(acc_ref)

    a = jnp.maximum(jnp.dot(h_ref[...], w1_ref[...], preferred_element_type=F32), 0.0)
    acc_ref[...] += jnp.dot((a * a).astype(BF16), w2_ref[...], preferred_element_type=F32)

    @pl.when(j == pl.num_programs(2) - 1)
    def _():
        row = t * tm + lax.broadcasted_iota(jnp.int32, (tm, 1), 0)
        gate = jnp.where(row >= n_lat, gc_ref[0, 0], gl_ref[0, 0])
        o_ref[0] = x_ref[0] + gate * acc_ref[...]


def _mlp(x, mod, layer, ctx_row, g, w1_bf, w2_bf, n_lat):
    bsz, seq, d = x.shape
    ff = w1_bf.shape[1]
    tm = _pick(seq, (768, 512, 256))
    tf = _pick(ff, (2048, 1024, 512, 128))
    return pl.pallas_call(
        functools.partial(_mlp_kernel, n_lat=n_lat, tm=tm),
        grid=(bsz, seq // tm, ff // tf),
        in_specs=[pl.BlockSpec((1, tm, d), lambda b, t, j: (b, t, 0))]
        + _mod_specs(layer, ctx_row, 3, 4, d, 3)
        + [pl.BlockSpec((1, 1, 1, d), lambda b, t, j: (layer, b, 0, 5)),
           pl.BlockSpec((1, 1, 1, d), lambda b, t, j: (layer, ctx_row, 0, 5)),
           pl.BlockSpec((1, d), lambda b, t, j: (0, 0)),
           pl.BlockSpec((d, tf), lambda b, t, j: (0, j)),
           pl.BlockSpec((tf, d), lambda b, t, j: (j, 0))],
        out_specs=pl.BlockSpec((1, tm, d), lambda b, t, j: (b, t, 0)),
        out_shape=jax.ShapeDtypeStruct((bsz, seq, d), F32),
        scratch_shapes=[pltpu.VMEM((tm, d), BF16), pltpu.VMEM((tm, d), F32)],
        compiler_params=pltpu.CompilerParams(dimension_semantics=("parallel", "parallel", "arbitrary")),
        name="mlp",
    )(x, mod, mod, mod, mod, mod, mod, g.reshape(1, d), w1_bf, w2_bf)


def kernel(x, c, ctx, c_ctx, ada_w, ada_b, norm1_g, norm2_g, w_in, q_norm_g, k_norm_g, hg_lb_raw, hg_norm_g,
           hy_short_w, hy_short_b, hy_filt_w1, hy_filt_b1, hy_filt_w2, hy_filt_b2, hy_filt_w3, hy_freq, hy_bias,
           w_branch, w_out, w_mlp1, w_mlp2):
    bsz, n_lat, d = x.shape
    n_ctx = ctx.shape[1]
    depth = ada_w.shape[0]
    gate_cols = 3 * d

    cvec = jnp.concatenate([c, c_ctx[None, :]], axis=0)
    ctx_row = bsz
    pad_rows = -cvec.shape[0] % 8
    cvec = jnp.pad(cvec, ((0, pad_rows), (0, 0)))
    mod = _adaln(cvec, ada_w, ada_b)
    mod = mod.reshape(depth, cvec.shape[0], 1, 6 * d)

    cos, sin_signed = _rope_tables(n_lat, n_ctx)
    filt_ctx, dft_ctx = _dft_matrices(n_ctx)
    filt_lat, mats, mats_t = _radix8_matrices(n_lat)

    xs = jnp.concatenate([x, ctx], axis=1)
    for l in range(depth):
        need_ctx = l < depth - 1
        w_l = w_in[l]
        w_bf = jnp.concatenate([w_l[:, -gate_cols:], w_l[:, :-gate_cols]], axis=1).astype(BF16)
        z = _inproj(xs, mod, l, ctx_row, norm1_g[l], w_bf, n_lat)

        att = _attention(z, cos, sin_signed, q_norm_g[l], k_norm_g[l], n_lat, need_ctx)
        hg = _hgrn(z, hg_lb_raw, hg_norm_g[l], l, n_lat, n_ctx, need_ctx)
        w1p = jnp.pad(hy_filt_w1[l], ((0, LANES - HY_EMB), (0, 0)))
        filt_args = (w1p, hy_filt_b1[l], hy_filt_w2[l], hy_filt_b2[l], hy_filt_w3[l], hy_freq[l])
        hy = _hyena_radix8(z, hy_short_w[l], hy_short_b[l], _hyena_filters(n_lat, filt_lat, *filt_args),
                           mats, mats_t, hy_bias[l], n_lat)
        if need_ctx:
            hy_ctx = _hyena_dense(z, hy_short_w[l], hy_short_b[l], _hyena_filters(n_ctx, filt_ctx, *filt_args),
                                  dft_ctx, hy_bias[l], n_ctx, n_lat // n_ctx)
            hy = jnp.concatenate([hy, hy_ctx], axis=1)
        rows = n_lat + n_ctx if need_ctx else n_lat
        xs = _merge(xs, att, hg, hy, z, mod, l, ctx_row, w_branch[l].astype(BF16), w_out[l].astype(BF16), n_lat,
                    rows)
        xs = _mlp(xs, mod, l, ctx_row, norm2_g[l], w_mlp1[l].astype(BF16), w_mlp2[l].astype(BF16), n_lat)
    return xs
```

```python
import functools
import math

import jax
import jax.numpy as jnp
import numpy as np
from jax import lax
from jax.experimental import pallas as pl
from jax.experimental.pallas import tpu as pltpu

F32 = jnp.float32
BF16 = jnp.bfloat16

LANES = 128
HEAD_DIM = 128
ATT_HEADS = 8
ATT_KV_HEADS = 2
ATT_GROUP = ATT_HEADS // ATT_KV_HEADS
ROPE_THETA = 10000.0
GRID_W = 64
HG_HEADS = 8
HG_CHUNK = 64
HG_MIN_F = 1e-6
HG_HEADS_PER_STEP = 4
HG_CHAIN_GROUP = 8
HY_BANDS = 16
HY_EMB = 1 + 2 * HY_BANDS
HY_FAST_DECAY = 0.3
HY_SLOW_DECAY = 1.5
HY_TARGET = 1e-2
HY_MIN_DECAY = math.log(HY_TARGET) / HY_SLOW_DECAY
HY_MAX_DECAY = math.log(HY_TARGET) / HY_FAST_DECAY
EPS = 1e-6
DFT_P = 256
HY_FWD_ROWS = 1024
HY_INV_ROWS = 512
HY_TILE = 512
HY_LANE_GROUP = 256
HIGHEST = lax.Precision.HIGHEST

COL_GATES = 0
COL_ATT_Q = 24
COL_ATT_K = 32
COL_ATT_V = 34
COL_HG_Q = 36
COL_HG_F = 44
COL_HG_B = 52
COL_HG_I = 60
COL_HG_G = 68
COL_HY = 76
N_COLS = 100


def _pick(n, candidates):
    for c in candidates:
        if n % c == 0:
            return c
    raise ValueError(f"no tile in {candidates} divides {n}")


def _nt_dot(a, b):
    return lax.dot_general(a, b, (((1,), (1,)), ((), ())), preferred_element_type=F32)


def _adaln_kernel(c_ref, w_ref, b_ref, o_ref):
    cv = c_ref[...]
    s = cv * jax.nn.sigmoid(cv)
    o_ref[0] = jnp.dot(s, w_ref[0], precision=HIGHEST, preferred_element_type=F32) + b_ref[0]


def _adaln(cvec, ada_w, ada_b):
    depth, d, n6 = ada_w.shape
    rows = cvec.shape[0]
    tn = _pick(n6, (1536, 1024, 512, 128))
    return pl.pallas_call(
        _adaln_kernel,
        grid=(depth, n6 // tn),
        in_specs=[pl.BlockSpec((rows, d), lambda l, j: (0, 0)),
                  pl.BlockSpec((1, d, tn), lambda l, j: (l, 0, j)),
                  pl.BlockSpec((1, 1, tn), lambda l, j: (l, 0, j))],
        out_specs=pl.BlockSpec((1, rows, tn), lambda l, j: (l, 0, j)),
        out_shape=jax.ShapeDtypeStruct((depth, rows, n6), F32),
        name="adaln",
    )(cvec, ada_w, ada_b.reshape(depth, 1, n6))


def _modulated_norm(x, g, sh_l, sc_l, sh_c, sc_c, row0, n_lat):
    y = x * lax.rsqrt(jnp.mean(x * x, axis=-1, keepdims=True) + EPS) * g
    row = row0 + lax.broadcasted_iota(jnp.int32, (x.shape[0], 1), 0)
    is_ctx = row >= n_lat
    sc = jnp.where(is_ctx, sc_c, sc_l)
    sh = jnp.where(is_ctx, sh_c, sh_l)
    return y * (1.0 + sc) + sh


def _mod_specs(layer, ctx_row, k_shift, k_scale, d, nargs):
    def spec(row_fn, k):
        if nargs == 3:
            return pl.BlockSpec((1, 1, 1, d), lambda b, t, j: (layer, row_fn(b), 0, k))
        return pl.BlockSpec((1, 1, 1, d), lambda b, t: (layer, row_fn(b), 0, k))
    lat = lambda b: b
    ctx = lambda b: ctx_row
    return [spec(lat, k_shift), spec(lat, k_scale), spec(ctx, k_shift), spec(ctx, k_scale)]


def _inproj_kernel(x_ref, shl_ref, scl_ref, shc_ref, scc_ref, g_ref, w_ref, o_ref, h_ref, *, n_lat, tm):
    t = pl.program_id(1)

    @pl.when(pl.program_id(2) == 0)
    def _():
        h = _modulated_norm(x_ref[0], g_ref[...], shl_ref[0, 0], scl_ref[0, 0], shc_ref[0, 0], scc_ref[0, 0],
                            t * tm, n_lat)
        h_ref[...] = h.astype(BF16)

    o_ref[0] = jnp.dot(h_ref[...], w_ref[...], preferred_element_type=F32).astype(BF16)


def _inproj(x, mod, layer, ctx_row, g, w_bf, n_lat):
    bsz, seq, d = x.shape
    n = w_bf.shape[1]
    tm = _pick(seq, (768, 512, 256))
    tn = _pick(n, (3200, 2560, 1280, 640, 128))
    return pl.pallas_call(
        functools.partial(_inproj_kernel, n_lat=n_lat, tm=tm),
        grid=(bsz, seq // tm, n // tn),
        in_specs=[pl.BlockSpec((1, tm, d), lambda b, t, j: (b, t, 0))]
        + _mod_specs(layer, ctx_row, 0, 1, d, 3)
        + [pl.BlockSpec((1, d), lambda b, t, j: (0, 0)),
           pl.BlockSpec((d, tn), lambda b, t, j: (0, j))],
        out_specs=pl.BlockSpec((1, tm, tn), lambda b, t, j: (b, t, j)),
        out_shape=jax.ShapeDtypeStruct((bsz, seq, n), BF16),
        scratch_shapes=[pltpu.VMEM((tm, d), BF16)],
        compiler_params=pltpu.CompilerParams(dimension_semantics=("parallel", "parallel", "arbitrary")),
        name="inproj",
    )(x, mod, mod, mod, mod, g.reshape(1, d), w_bf)


def _rope(x, cos, sin_signed):
    lane = lax.broadcasted_iota(jnp.int32, x.shape, 1)
    first = (lane % 64) < 32
    rx = jnp.where(first, pltpu.roll(x, 96, 1), pltpu.roll(x, 32, 1))
    return x * cos + rx * sin_signed


def _head_norm(x, g):
    return x * lax.rsqrt(jnp.mean(x * x, axis=-1, keepdims=True) + EPS) * g


def _attn_kernel(q_ref, k_ref, v_ref, cq_ref, sq_ref, ck_ref, sk_ref, gq_ref, gk_ref, o_ref, ks_ref, vs_ref,
                 *, n_lat, tq):
    qi = pl.program_id(2)

    @pl.when(qi == 0)
    def _():
        k = _head_norm(k_ref[0].astype(F32), gk_ref[...])
        ks_ref[...] = _rope(k, ck_ref[...], sk_ref[...]).astype(BF16)
        vs_ref[:, :HEAD_DIM] = v_ref[0]
        vs_ref[:, HEAD_DIM:] = jnp.ones((vs_ref.shape[0], HEAD_DIM), BF16)

    scale = HEAD_DIM ** -0.5 * math.log2(math.e)

    def attend(keys, vals):
        qall = q_ref[0].astype(F32)
        qs = []
        for g in range(ATT_GROUP):
            qg = _head_norm(qall[:, g * HEAD_DIM:(g + 1) * HEAD_DIM], gq_ref[...])
            qs.append((_rope(qg, cq_ref[...], sq_ref[...]) * scale).astype(BF16))
        ss = [_nt_dot(q, keys) for q in qs]
        ps = [jnp.exp2(s - jnp.max(s, axis=-1, keepdims=True)).astype(BF16) for s in ss]
        for g, p in enumerate(ps):
            o2 = jnp.dot(p, vals, preferred_element_type=F32)
            o = o2[:, :HEAD_DIM] / o2[:, HEAD_DIM:]
            o_ref[0, :, g * HEAD_DIM:(g + 1) * HEAD_DIM] = o.astype(BF16)

    is_lat = qi * tq < n_lat

    @pl.when(is_lat)
    def _():
        attend(ks_ref[...], vs_ref[...])

    @pl.when(jnp.logical_not(is_lat))
    def _():
        attend(ks_ref[n_lat:, :], vs_ref[n_lat:, :])


def _attention(z, cos, sin_signed, gq, gk, n_lat, need_ctx):
    bsz, seq, _ = z.shape
    tq = 256
    n_q = (seq if need_ctx else n_lat) // tq
    gw = ATT_GROUP * HEAD_DIM
    return pl.pallas_call(
        functools.partial(_attn_kernel, n_lat=n_lat, tq=tq),
        grid=(bsz, ATT_KV_HEADS, n_q),
        in_specs=[pl.BlockSpec((1, tq, gw), lambda b, h, i: (b, i, COL_ATT_Q // ATT_GROUP + h)),
                  pl.BlockSpec((1, seq, HEAD_DIM), lambda b, h, i: (b, 0, COL_ATT_K + h)),
                  pl.BlockSpec((1, seq, HEAD_DIM), lambda b, h, i: (b, 0, COL_ATT_V + h)),
                  pl.BlockSpec((tq, HEAD_DIM), lambda b, h, i: (i, 0)),
                  pl.BlockSpec((tq, HEAD_DIM), lambda b, h, i: (i, 0)),
                  pl.BlockSpec((seq, HEAD_DIM), lambda b, h, i: (0, 0)),
                  pl.BlockSpec((seq, HEAD_DIM), lambda b, h, i: (0, 0)),
                  pl.BlockSpec((1, HEAD_DIM), lambda b, h, i: (0, 0)),
                  pl.BlockSpec((1, HEAD_DIM), lambda b, h, i: (0, 0))],
        out_specs=pl.BlockSpec((1, tq, gw), lambda b, h, i: (b, i, h)),
        out_shape=jax.ShapeDtypeStruct((bsz, n_q * tq, ATT_HEADS * HEAD_DIM), BF16),
        scratch_shapes=[pltpu.VMEM((seq, HEAD_DIM), BF16), pltpu.VMEM((seq, 2 * HEAD_DIM), BF16)],
        compiler_params=pltpu.CompilerParams(dimension_semantics=("parallel", "parallel", "arbitrary")),
        name="attention",
    )(z, z, z, cos, sin_signed, cos, sin_signed, gq.reshape(1, HEAD_DIM), gk.reshape(1, HEAD_DIM))


def _rope_tables(n_lat, n_ctx):
    half = HEAD_DIM // 2
    inv = ROPE_THETA ** (-np.arange(0, half, 2, dtype=np.float64) / half)
    t = np.arange(n_lat)
    ar = (t // GRID_W)[:, None] * inv
    ac = (t % GRID_W)[:, None] * inv
    ang = np.concatenate([ar, ar, ac, ac], axis=-1)
    cos = np.concatenate([np.cos(ang), np.ones((n_ctx, HEAD_DIM))], axis=0)
    sin = np.concatenate([np.sin(ang), np.zeros((n_ctx, HEAD_DIM))], axis=0)
    sign = np.where((np.arange(HEAD_DIM) % 64) < 32, -1.0, 1.0)
    return jnp.asarray(cos, F32), jnp.asarray(sin * sign, F32)


def _hgrn_masks():
    c = HG_CHUNK
    idx = np.arange(c)
    t, s = idx[:, None], idx[None, :]

    def halving(size):
        half = size // 2
        return (t // size == s // size) & (t % size >= half) & (s % size < half)

    diag = (t // 8 == s // 8) & (s <= t)
    flip = lambda m: m[::-1, ::-1]
    fwd = np.stack([np.concatenate([halving(64), halving(32)], axis=1),
                    np.concatenate([halving(16), diag], axis=1)])
    bwd = np.stack([np.concatenate([flip(halving(64)), flip(halving(32))], axis=1),
                    np.concatenate([flip(halving(16)), flip(diag)], axis=1)])
    return jnp.asarray(np.stack([fwd, bwd]), F32)


def _hgrn_exponents(zg, lb, reverse):
    tile = 8
    nt = HG_CHUNK // tile
    sig = jax.nn.sigmoid(zg.astype(F32))
    f = sig if lb is None else lb + (1.0 - lb) * sig
    lf = jnp.log2(jnp.maximum(f, HG_MIN_F))
    kk = 1.0 - f
    p = lf.reshape(nt, tile, LANES)
    sub = lax.broadcasted_iota(jnp.int32, p.shape, 1)
    for k in (1, 2, 4):
        if reverse:
            p = p + jnp.where(sub < tile - k, pltpu.roll(p, tile - k, 1), 0.0)
        else:
            p = p + jnp.where(sub >= k, pltpu.roll(p, k, 1), 0.0)
    order = list(range(nt - 1, -1, -1)) if reverse else list(range(nt))
    tot_row, mid_row = (0, 4) if reverse else (tile - 1, 3)
    s = [p[j] for j in order]
    tb = [jnp.broadcast_to(p[j, tot_row:tot_row + 1, :], (tile, LANES)) for j in order]
    mb = [jnp.broadcast_to(p[j, mid_row:mid_row + 1, :], (tile, LANES)) for j in order]
    t01, t23, t45 = tb[0] + tb[1], tb[2] + tb[3], tb[4] + tb[5]
    t123, t456 = tb[1] + t23, t45 + tb[6]
    c4 = t01 + t23
    before = [None, tb[0], t01, t01 + tb[2], c4, c4 + tb[4], c4 + t45, c4 + t456]
    total = before[7] + tb[7]
    cum = [s[0]] + [before[k] + s[k] for k in range(1, nt)]
    suf = [total - cum[k] for k in range(nt)]
    w8 = [s[k] - mb[k] for k in range(nt)]
    w16 = [s[k] if k % 2 else tb[k] - s[k] for k in range(nt)]
    w32 = [t01 - s[0], tb[1] - s[1], s[2], tb[2] + s[3], t45 - s[4], tb[5] - s[5], s[6], tb[6] + s[7]]
    w64 = [c4 - s[0], t123 - s[1], t23 - s[2], tb[3] - s[3], s[4], tb[4] + s[5], t45 + s[6], t456 + s[7]]

    def natural(parts):
        return jnp.concatenate([parts[order.index(j)] for j in range(nt)], axis=0)

    ex = {"cum": natural(cum), "suf": natural(suf), 64: natural(w64), 32: natural(w32), 16: natural(w16),
          8: natural(w8), "tot": total[0:1]}
    return kk, ex


def _hgrn_scores(q, kk, ex, st):
    qf = q.astype(F32)
    inter = _nt_dot((qf * jnp.exp2(ex["cum"])).astype(BF16), st.astype(BF16))
    zeros = jnp.zeros(q.shape, BF16)
    pairs = []
    for la, lb_ in ((64, 32), (16, 8)):
        ea, eb = jnp.exp2(ex[la]), jnp.exp2(ex[lb_])
        eb_k = jnp.exp2(-ex[lb_]) if lb_ == 8 else eb
        qa = jnp.concatenate([(qf * ea).astype(BF16), (qf * eb).astype(BF16)], axis=1)
        ka = jnp.concatenate([(kk * ea).astype(BF16), zeros], axis=1)
        kb = jnp.concatenate([zeros, (kk * eb_k).astype(BF16)], axis=1)
        pairs.append(_nt_dot(qa, jnp.concatenate([ka, kb], axis=0)))
    return inter, pairs


def _hgrn_output(inter, pairs, v, mask):
    p0 = pairs[0] * mask[0]
    p1 = jnp.where(mask[1] > 0.0, pairs[1], 0.0)
    probs = jnp.concatenate([p0.astype(BF16), p1.astype(BF16)], axis=1)
    return inter + jnp.dot(probs, jnp.concatenate([v, v, v, v], axis=0), preferred_element_type=F32)


def _hgrn_state(kk, ex, v, st):
    k_suf = (kk * jnp.exp2(ex["suf"])).astype(BF16)
    upd = lax.dot_general(v, k_suf, (((0,), (0,)), ((), ())), preferred_element_type=F32)
    return st * jnp.exp2(ex["tot"]) + upd


def _hgrn_kernel(q_ref, zf_ref, zb_ref, i_ref, g_ref, lbraw_ref, gn_ref, mask_ref, o_ref,
                 acc_ref, st_ref, *, layer, n_lat, n_ctx, need_ctx):
    c = HG_CHUNK
    nc_lat, nc_ctx = n_lat // c, n_ctx // c
    nc = nc_lat + nc_ctx
    raw = lbraw_ref[...].astype(F32)
    ew = jnp.exp(raw - jnp.max(raw, axis=0, keepdims=True))
    sm = ew / jnp.sum(ew, axis=0, keepdims=True)
    lower = jnp.zeros_like(sm[0])
    for dpt in range(1, layer + 1):
        lower = lower + sm[dpt]

    acc_ref[...] = jnp.zeros_like(acc_ref)
    st_ref[...] = jnp.zeros_like(st_ref)

    def step(n, carry, with_outputs=True):
        cf = jnp.where(n < nc_ctx, nc_lat + n, n - nc_ctx)
        cb = nc - 1 - n
        chains = []
        for hh in range(HG_HEADS_PER_STEP):
            lanes = slice(hh * LANES, (hh + 1) * LANES)
            for d, (chunk, z_ref) in enumerate(((cf, zf_ref), (cb, zb_ref))):
                rows = pl.ds(pl.multiple_of(chunk * c, c), c)
                chains.append((hh, d, rows, lanes, z_ref))
        for g0 in range(0, len(chains), HG_CHAIN_GROUP):
            group = chains[g0:g0 + HG_CHAIN_GROUP]
            gates = [_hgrn_exponents(z_ref[0, rows, lanes], None if layer == 0 else lower[hh, d:d + 1], d == 1)
                     for hh, d, rows, lanes, z_ref in group]
            if with_outputs:
                scored = [_hgrn_scores(q_ref[0, rows, lanes], kk, ex, st_ref[hh, d])
                          for (hh, d, rows, lanes, _), (kk, ex) in zip(group, gates)]
                for (hh, d, rows, lanes, _), (inter, pairs) in zip(group, scored):
                    out = _hgrn_output(inter, pairs, i_ref[0, rows, lanes], mask_ref[d])
                    acc_ref[rows, lanes] = acc_ref[rows, lanes] + out
            for (hh, d, rows, lanes, _), (kk, ex) in zip(group, gates):
                st_ref[hh, d] = _hgrn_state(kk, ex, i_ref[0, rows, lanes], st_ref[hh, d])
        return carry

    if need_ctx:
        lax.fori_loop(0, nc, step, 0)
    else:
        lax.fori_loop(0, nc_ctx, functools.partial(step, with_outputs=False), 0)
        lax.fori_loop(nc_ctx, nc, step, 0)

    rows_out = o_ref.shape[1]
    for hh in range(HG_HEADS_PER_STEP):
        lanes = slice(hh * LANES, (hh + 1) * LANES)
        o = acc_ref[:, lanes]
        y = o * lax.rsqrt(jnp.mean(o * o, axis=-1, keepdims=True) + EPS) * gn_ref[...]
        g = g_ref[0, :rows_out, lanes].astype(F32)
        o_ref[0, :, lanes] = (y * (g * jax.nn.sigmoid(g))).astype(BF16)


def _hgrn(z, lb_raw, g_norm, layer, n_lat, n_ctx, need_ctx):
    bsz, seq, _ = z.shape
    depth = lb_raw.shape[0]
    hps = HG_HEADS_PER_STEP
    width = hps * LANES
    masks = _hgrn_masks()
    rows_out = seq if need_ctx else n_lat
    lbr = lb_raw.reshape(depth, 2, HG_HEADS, LANES).transpose(0, 2, 1, 3)
    col = lambda c0: pl.BlockSpec((1, seq, width), lambda b, h: (b, 0, c0 // hps + h))
    return pl.pallas_call(
        functools.partial(_hgrn_kernel, layer=layer, n_lat=n_lat, n_ctx=n_ctx, need_ctx=need_ctx),
        grid=(bsz, HG_HEADS // hps),
        in_specs=[col(COL_HG_Q), col(COL_HG_F), col(COL_HG_B), col(COL_HG_I), col(COL_HG_G),
                  pl.BlockSpec((depth, hps, 2, LANES), lambda b, h: (0, h, 0, 0)),
                  pl.BlockSpec((1, LANES), lambda b, h: (0, 0)),
                  pl.BlockSpec((2, 2, HG_CHUNK, LANES), lambda b, h: (0, 0, 0, 0))],
        out_specs=pl.BlockSpec((1, rows_out, width), lambda b, h: (b, 0, h)),
        out_shape=jax.ShapeDtypeStruct((bsz, rows_out, HG_HEADS * LANES), BF16),
        scratch_shapes=[pltpu.VMEM((rows_out, width), F32), pltpu.VMEM((hps, 2, LANES, LANES), F32)],
        compiler_params=pltpu.CompilerParams(dimension_semantics=("parallel", "parallel")),
        name="hgrn",
    )(z, z, z, z, z, lbr, g_norm.reshape(1, LANES), masks)


def _dft_matrices(n):
    size = 2 * n
    k = np.arange(n)[:, None]
    t = np.arange(n)[None, :]
    ang = (2.0 * np.pi / size) * ((k * t) % size)
    cos, sin = np.cos(ang), np.sin(ang)
    sin[0, :] = np.where(np.arange(n) % 2 == 0, 1.0, -1.0)
    fwd = np.concatenate([cos.reshape(n // DFT_P, DFT_P, n), sin.reshape(n // DFT_P, DFT_P, n)], axis=1)
    fwd = fwd.reshape(size, n)
    pairs = [(cos[i:i + DFT_P], sin[i:i + DFT_P]) for i in range(0, n, DFT_P)]
    filt = _group_filter_rows(pairs, _filter_group(n))
    return jnp.asarray(filt, BF16), (jnp.asarray(fwd, BF16), jnp.asarray(fwd.T.copy(), BF16))


def _hyena_positions(n):
    t = np.linspace(0.0, 1.0, n)[:, None]
    w = (2.0 * math.pi / n) * np.arange(n)[:, None]
    f = np.linspace(1e-4, HY_BANDS - 1, HY_BANDS)[None, :]
    z = np.concatenate([t, np.cos(f * w), -np.sin(f * w)], axis=-1)
    return jnp.asarray(np.pad(z, ((0, 0), (0, LANES - HY_EMB))), F32)


def _filter_kernel(pos_ref, w1_ref, b1_ref, w2_ref, b2_ref, fr_ref, w3f_ref, w3b_ref, dl_ref, fs_ref, fd_ref,
                   h_ref):
    pos = pos_ref[...]

    @pl.when((pl.program_id(0) == 0) & (pl.program_id(1) == 0))
    def _():
        h1 = jnp.sin(fr_ref[0:1] * (jnp.dot(pos, w1_ref[...], precision=HIGHEST, preferred_element_type=F32)
                                    + b1_ref[...]))
        h_ref[...] = jnp.sin(fr_ref[1:2] * (jnp.dot(h1, w2_ref[...], precision=HIGHEST,
                                                    preferred_element_type=F32) + b2_ref[...]))

    h = h_ref[...]
    window = jnp.exp(-pos[:, 0:1] * dl_ref[...])
    hf = jnp.dot(h, w3f_ref[...], precision=HIGHEST, preferred_element_type=F32) * window
    hb = jnp.dot(h, w3b_ref[...], precision=HIGHEST, preferred_element_type=F32) * window
    row = lax.broadcasted_iota(jnp.int32, (pos.shape[0], 1), 0)
    hb = jnp.where(row == 0, 0.0, hb)
    norm = jnp.sum(jnp.abs(hf), axis=0, keepdims=True) + jnp.sum(jnp.abs(hb), axis=0, keepdims=True) + EPS
    inv = 1.0 / norm
    fs_ref[0] = ((hf + hb) * inv).astype(BF16)
    fd_ref[0] = ((hf - hb) * inv).astype(BF16)


def _filter_group(n):
    chunks = n // DFT_P
    return 4 if chunks % 4 == 0 else 1


def _group_filter_rows(pairs, group):
    out = []
    for g0 in range(0, len(pairs), group):
        out += [c for c, _ in pairs[g0:g0 + group]] + [s for _, s in pairs[g0:g0 + group]]
    return np.concatenate(out)


def _filter_dft_kernel(a_ref, fs_ref, fd_ref, o_ref, *, n, group):
    m = pl.program_id(2)
    p = DFT_P
    gp = group * p
    first = (lax.broadcasted_iota(jnp.int32, (gp, 1), 0) == 0) & (m == 0)
    weight = jnp.where(first, 1.0, 2.0) / (2 * n)
    hc = jnp.dot(a_ref[:gp], fs_ref[0], preferred_element_type=F32) * weight
    hs = jnp.dot(a_ref[gp:], fd_ref[0], preferred_element_type=F32) * weight
    for g in range(group):
        o_ref[0, 2 * p * g:2 * p * g + p] = hc[g * p:(g + 1) * p]
        o_ref[0, 2 * p * g + p:2 * p * (g + 1)] = hs[g * p:(g + 1) * p]

    @pl.when(m == 0)
    def _():
        hny = jnp.dot(a_ref[gp:gp + 16], fs_ref[0], preferred_element_type=F32)[:8]
        o_ref[0, p:p + 8] = jnp.where(first[:8], hny / (2 * n), hs[:8])


def _hyena_filters(n, fwd, w1p, b1, w2, b2, w3, freq):
    hid = w2.shape[0]
    c = w3.shape[1] // 4
    tn = _pick(c, (512, 256, 128))
    nct = c // tn
    pos = _hyena_positions(n)
    deltas = jnp.asarray(np.abs(np.linspace(HY_MIN_DECAY, HY_MAX_DECAY, c))[None, :], F32)
    full = lambda shape: pl.BlockSpec(shape, lambda o, j: (0,) * len(shape))
    fsum, fdiff = pl.pallas_call(
        _filter_kernel,
        grid=(2, nct),
        in_specs=[full((n, LANES)), full((LANES, hid)), full((1, hid)), full((hid, hid)), full((1, hid)),
                  full((2, hid)),
                  pl.BlockSpec((hid, tn), lambda o, j: (0, (2 * o) * nct + j)),
                  pl.BlockSpec((hid, tn), lambda o, j: (0, (2 * o + 1) * nct + j)),
                  pl.BlockSpec((1, tn), lambda o, j: (0, j))],
        out_specs=[pl.BlockSpec((1, n, tn), lambda o, j: (o, 0, j))] * 2,
        out_shape=[jax.ShapeDtypeStruct((2, n, c), BF16)] * 2,
        scratch_shapes=[pltpu.VMEM((n, hid), F32)],
        compiler_params=pltpu.CompilerParams(dimension_semantics=("arbitrary", "arbitrary")),
        name="hyena_filter",
    )(pos, w1p, b1.reshape(1, hid), w2, b2.reshape(1, hid), freq, w3, w3, deltas)
    group = _filter_group(n)
    rows = 2 * DFT_P * group
    return pl.pallas_call(
        functools.partial(_filter_dft_kernel, n=n, group=group),
        grid=(2, nct, 2 * n // rows),
        in_specs=[pl.BlockSpec((rows, n), lambda o, j, m: (m, 0)),
                  pl.BlockSpec((1, n, tn), lambda o, j, m: (o, 0, j)),
                  pl.BlockSpec((1, n, tn), lambda o, j, m: (o, 0, j))],
        out_specs=pl.BlockSpec((1, rows, tn), lambda o, j, m: (o, m, j)),
        out_shape=jax.ShapeDtypeStruct((2, 2 * n, c), F32),
        name="hyena_filter_dft",
    )(fwd, fsum, fdiff)


def _short_conv(u, w, b):
    n = u.shape[0]
    row = lax.broadcasted_iota(jnp.int32, (n, 1), 0)
    prev = jnp.where(row == 0, 0.0, pltpu.roll(u, 1, 0))
    nxt = jnp.where(row == n - 1, 0.0, pltpu.roll(u, n - 1, 0))
    return prev * w[0:1] + u * w[1:2] + nxt * w[2:3] + b


def _shortconv_kernel(u_ref, w_ref, b_ref, o_ref):
    o_ref[0] = _short_conv(u_ref[0].astype(F32), w_ref[...], b_ref[...]).astype(BF16)


def _shortconv(z, w, b, n, row_blk):
    bsz = z.shape[0]
    width = w.shape[1]
    tn = HY_TILE
    assert (COL_HY * LANES) % tn == 0 and width % tn == 0
    c0 = COL_HY * LANES // tn
    return pl.pallas_call(
        _shortconv_kernel,
        grid=(bsz, width // tn),
        in_specs=[pl.BlockSpec((1, n, tn), lambda bb, j: (bb, row_blk, c0 + j)),
                  pl.BlockSpec((3, tn), lambda bb, j: (0, j)),
                  pl.BlockSpec((1, tn), lambda bb, j: (0, j))],
        out_specs=pl.BlockSpec((1, n, tn), lambda bb, j: (bb, 0, j)),
        out_shape=jax.ShapeDtypeStruct((bsz, n, width), BF16),
        compiler_params=pltpu.CompilerParams(dimension_semantics=("parallel", "parallel")),
        name="hyena_shortconv",
    )(z, w, b.reshape(1, width))


def _conv_fwd_kernel(a_ref, v_ref, h_ref, y_ref, *, chunks):
    p = DFT_P
    zf = jnp.dot(a_ref[...], v_ref[0], preferred_element_type=F32)
    for k in range(chunks):
        lo = 2 * p * k
        zc, zs = zf[lo:lo + p], zf[lo + p:lo + 2 * p]
        hc, hs = h_ref[0, lo:lo + p], h_ref[0, lo + p:lo + 2 * p]
        yc, ys = zc * hc - zs * hs, zc * hs + zs * hc
        if k == 0:
            real_row = (lax.broadcasted_iota(jnp.int32, (p, 1), 0) == 0) & (pl.program_id(1) == 0)
            yc = jnp.where(real_row, zc * hc, yc)
            ys = jnp.where(real_row, zs * hs, ys)
        y_ref[0, lo:lo + p] = yc.astype(BF16)
        y_ref[0, lo + p:lo + 2 * p] = ys.astype(BF16)


def _conv_fwd(fwd, v_arr, v_col, spec, order, n, row_blk):
    bsz = v_arr.shape[0]
    c = spec.shape[2]
    rows = min(HY_FWD_ROWS, 2 * n)
    return pl.pallas_call(
        functools.partial(_conv_fwd_kernel, chunks=rows // (2 * DFT_P)),
        grid=(bsz, 2 * n // rows),
        in_specs=[pl.BlockSpec((rows, n), lambda b, m: (m, 0)),
                  pl.BlockSpec((1, n, c), lambda b, m: (b, row_blk, v_col)),
                  pl.BlockSpec((1, rows, c), lambda b, m: (order, m, 0))],
        out_specs=pl.BlockSpec((1, rows, c), lambda b, m: (b, m, 0)),
        out_shape=jax.ShapeDtypeStruct((bsz, 2 * n, c), BF16),
        compiler_params=pltpu.CompilerParams(dimension_semantics=("parallel", "arbitrary")),
        name="hyena_conv_fwd",
    )(fwd, v_arr, spec)


def _conv_inv_kernel(g_ref, y_ref, gate_ref, zin_ref, bias_ref, o_ref):
    y = jnp.dot(g_ref[...], y_ref[0], preferred_element_type=F32)
    o_ref[0] = (gate_ref[0].astype(F32) * (y + zin_ref[0].astype(F32) * bias_ref[0])).astype(BF16)


def _conv_inv(inv, y, gate, zin, bias, order, n):
    bsz, _, c = y.shape
    tm = min(HY_INV_ROWS, n)

    def rows_of(triple):
        _, row0, col = triple
        return pl.BlockSpec((1, tm, c), lambda b, m: (b, row0 // tm + m, col))

    return pl.pallas_call(
        _conv_inv_kernel,
        grid=(bsz, n // tm),
        in_specs=[pl.BlockSpec((tm, 2 * n), lambda b, m: (m, 0)),
                  pl.BlockSpec((1, 2 * n, c), lambda b, m: (b, 0, 0)),
                  rows_of(gate), rows_of(zin),
                  pl.BlockSpec((1, 1, c), lambda b, m: (order, 0, 0))],
        out_specs=pl.BlockSpec((1, tm, c), lambda b, m: (b, m, 0)),
        out_shape=jax.ShapeDtypeStruct((bsz, n, c), BF16),
        compiler_params=pltpu.CompilerParams(dimension_semantics=("parallel", "arbitrary")),
        name="hyena_conv_inv",
    )(inv, y, gate[0], zin[0], bias.reshape(2, 1, c))


def _hyena_dense(z, sw, sb, spectrum, dft, bias, n, row_blk):
    u = _shortconv(z, sw, sb, n, row_blk)
    x1_col, x2_col, v_col = 0, 1, 2
    fwd, inv = dft
    y = _conv_fwd(fwd, u, v_col, spectrum, 0, n, 0)
    z1 = _conv_inv(inv, y, (u, 0, x1_col), (u, 0, v_col), bias, 0, n)
    y = _conv_fwd(fwd, z1, 0, spectrum, 1, n, 0)
    return _conv_inv(inv, y, (u, 0, x2_col), (z1, 0, 0), bias, 1, n)


def _radix8_frequencies(n):
    p = n // 8
    j = np.arange(p)
    return [8 * j, 4 + 8 * j, 1 + 8 * j, 1 + 8 * (j + p), 2 + 8 * j, 2 + 8 * (j + p), 3 + 8 * j, 3 + 8 * (j + p)]


def _radix8_matrices(n):
    size, p = 2 * n, n // 8
    freqs = _radix8_frequencies(n)

    def cos_sin(k, length):
        ang = (2.0 * np.pi / size) * ((k[:, None] * np.arange(length)[None, :]) % size)
        return np.cos(ang), np.sin(ang)

    def nyquist(sin_rows, length):
        sin_rows[0, :] = np.where(np.arange(length) % 2 == 0, 1.0, -1.0)
        return sin_rows

    pairs = []
    for idx, k in enumerate(freqs):
        c, s = cos_sin(k, n)
        pairs.append((c, nyquist(s, n) if idx == 0 else s))
    filt = _group_filter_rows(pairs, _filter_group(n))
    slab = size // 8
    c0, s0 = cos_sin(freqs[0], slab)
    c4, s4 = cos_sin(freqs[1], slab)
    mats = [np.concatenate([c0, nyquist(s0, slab)]), np.concatenate([c4, s4])]
    for k1 in (1, 2, 3):
        ca, sa = cos_sin(freqs[2 * k1], slab)
        cb, sb = cos_sin(freqs[2 * k1 + 1], slab)
        mats.append(np.block([[ca, sa], [sa, -ca], [cb, sb], [sb, -cb]]))
    to_bf = lambda a: jnp.asarray(a, BF16)
    return to_bf(filt), [to_bf(m) for m in mats], [to_bf(m.T.copy()) for m in mats]


def _conv_fwd8_kernel(*refs, pre_conv):
    if pre_conv:
        v_ref, w_ref, b_ref, r0_ref, r4_ref, r1_ref, r2_ref, r3_ref, h_ref, y_ref, vs_ref = refs
    else:
        v_ref, r0_ref, r4_ref, r1_ref, r2_ref, r3_ref, h_ref, y_ref = refs
    n, tn = v_ref.shape[1], v_ref.shape[2]
    slab, p = n // 4, n // 8
    bf = lambda x: x.astype(BF16)
    cat = lambda re, im: jnp.concatenate([bf(re), bf(im)], axis=0)
    groups = [slice(g * HY_LANE_GROUP, (g + 1) * HY_LANE_GROUP) for g in range(tn // HY_LANE_GROUP)]
    inputs = []
    for lanes in groups:
        v = v_ref[0, :, lanes].astype(F32)
        if pre_conv:
            v = _short_conv(v, w_ref[:, lanes], b_ref[:, lanes])
            vs_ref[0, :, lanes] = v.astype(BF16)
        z0, z1, z2, z3 = (v[i * slab:(i + 1) * slab] for i in range(4))
        e, o = z0 + z2, z1 + z3
        a, b = (z1 - z3) * (0.5 ** 0.5), o * (0.5 ** 0.5)
        inputs.append((bf(e + o), bf(e - o), cat(z0 + a, -z2 - b), cat(z0 - z2, z3 - z1), cat(z0 - a, z2 - b)))
    spectra = [[jnp.dot(r[...], x, preferred_element_type=F32)
                for r, x in zip((r0_ref, r4_ref, r1_ref, r2_ref, r3_ref), xs)] for xs in inputs]
    for lanes, spec in zip(groups, spectra):
        chunks = [spec[0], spec[1]] + [s[i * 2 * p:(i + 1) * 2 * p] for s in spec[2:] for i in range(2)]
        for c, zf in enumerate(chunks):
            lo = 2 * p * c
            zc, zs = zf[:p], zf[p:]
            hc, hs = h_ref[0, lo:lo + p, lanes], h_ref[0, lo + p:lo + 2 * p, lanes]
            yc, ys = zc * hc - zs * hs, zc * hs + zs * hc
            if c == 0:
                real_row = lax.broadcasted_iota(jnp.int32, (p, 1), 0) == 0
                yc = jnp.where(real_row, zc * hc, yc)
                ys = jnp.where(real_row, zs * hs, ys)
            y_ref[0, lo:lo + p, lanes] = yc.astype(BF16)
            y_ref[0, lo + p:lo + 2 * p, lanes] = ys.astype(BF16)


def _const_spec(shape):
    return pl.BlockSpec(shape, lambda j, b: (0,) * len(shape), pipeline_mode=pl.Buffered(1))


def _conv_fwd8(mats, v_arr, v_col, spec, order, n, short=None):
    bsz = v_arr.shape[0]
    c = spec.shape[2]
    tn = HY_TILE
    in_specs = [pl.BlockSpec((1, n, tn), lambda j, b: (b, 0, v_col + j))]
    args = [v_arr]
    out_specs = [pl.BlockSpec((1, 2 * n, tn), lambda j, b: (b, 0, j))]
    out_shape = [jax.ShapeDtypeStruct((bsz, 2 * n, c), BF16)]
    if short is not None:
        w, bvec, wcol = short
        in_specs += [pl.BlockSpec((3, tn), lambda j, b: (0, wcol + j)),
                     pl.BlockSpec((1, tn), lambda j, b: (0, wcol + j))]
        args += [w, bvec.reshape(1, -1)]
        out_specs.append(pl.BlockSpec((1, n, tn), lambda j, b: (b, 0, j)))
        out_shape.append(jax.ShapeDtypeStruct((bsz, n, c), BF16))
    in_specs += [_const_spec(m.shape) for m in mats]
    in_specs.append(pl.BlockSpec((1, 2 * n, tn), lambda j, b: (order, 0, j)))
    res = pl.pallas_call(
        functools.partial(_conv_fwd8_kernel, pre_conv=short is not None),
        grid=(c // tn, bsz),
        in_specs=in_specs,
        out_specs=out_specs,
        out_shape=out_shape,
        compiler_params=pltpu.CompilerParams(dimension_semantics=("parallel", "parallel")),
        name="hyena_conv_fwd8",
    )(*args, *mats, spec)
    return res if short is not None else res[0]


def _conv_inv8_kernel(y_ref, r0_ref, r4_ref, r1_ref, r2_ref, r3_ref, gate_ref, w_ref, b_ref, zin_ref, bias_ref,
                      *rest):
    o_ref = rest[-1]
    n, tn = gate_ref.shape[1], gate_ref.shape[2]
    if len(rest) == 2:
        o_ref[0, n:, :] = rest[0][0]
    slab = n // 4
    groups = [slice(g * HY_LANE_GROUP, (g + 1) * HY_LANE_GROUP) for g in range(tn // HY_LANE_GROUP)]
    bounds = (0, slab, 2 * slab, 4 * slab, 6 * slab, 8 * slab)
    gates = [_short_conv(gate_ref[0, :, lanes].astype(F32), w_ref[:, lanes], b_ref[:, lanes]) for lanes in groups]
    skips = [zin_ref[0, :, lanes].astype(F32) * bias_ref[0, :, lanes] for lanes in groups]
    parts = [[jnp.dot(r[...], y_ref[0, lo:hi, lanes], preferred_element_type=F32)
              for r, lo, hi in zip((r0_ref, r4_ref, r1_ref, r2_ref, r3_ref), bounds[:-1], bounds[1:])]
             for lanes in groups]
    for lanes, gate, skip, (v0, v4, v1, v2, v3) in zip(groups, gates, skips, parts):
        (v1r, v1i), (v2r, v2i), (v3r, v3i) = ((v[:slab], v[slab:]) for v in (v1, v2, v3))
        s, d = v0 + v4, v0 - v4
        r = 0.5 ** 0.5
        outs = [s + v1r + v2r + v3r,
                d + (v1r - v1i - v3r - v3i) * r - v2i,
                s - v1i - v2r + v3i,
                d + (v3r - v3i - v1r - v1i) * r + v2i]
        for i, y in enumerate(outs):
            rows = slice(i * slab, (i + 1) * slab)
            o_ref[0, rows, lanes] = (gate[rows] * (y + skip[rows])).astype(BF16)


def _conv_inv8(mats_t, y, gate_arr, gate_col, short, zin, bias, order, n, tail=None):
    bsz, _, c = y.shape
    tn = HY_TILE
    w, bvec, wcol = short
    rows_out = n if tail is None else n + tail.shape[1]
    tail_specs = [] if tail is None else [pl.BlockSpec((1, tail.shape[1], tn), lambda j, b: (b, 0, j))]
    tail_args = [] if tail is None else [tail]
    return pl.pallas_call(
        _conv_inv8_kernel,
        grid=(c // tn, bsz),
        in_specs=[pl.BlockSpec((1, 2 * n, tn), lambda j, b: (b, 0, j))]
        + [_const_spec(m.shape) for m in mats_t]
        + [pl.BlockSpec((1, n, tn), lambda j, b: (b, 0, gate_col + j)),
           pl.BlockSpec((3, tn), lambda j, b: (0, wcol + j)),
           pl.BlockSpec((1, tn), lambda j, b: (0, wcol + j)),
           pl.BlockSpec((1, n, tn), lambda j, b: (b, 0, j)),
           pl.BlockSpec((1, 1, tn), lambda j, b: (order, 0, j))] + tail_specs,
        out_specs=pl.BlockSpec((1, rows_out, tn), lambda j, b: (b, 0, j)),
        out_shape=jax.ShapeDtypeStruct((bsz, rows_out, c), BF16),
        compiler_params=pltpu.CompilerParams(dimension_semantics=("parallel", "parallel")),
        name="hyena_conv_inv8",
    )(y, *mats_t, gate_arr, w, bvec.reshape(1, -1), zin, bias.reshape(2, 1, c), *tail_args)


def _hyena_radix8(z, sw, sb, spectrum, mats, mats_t, bias, n, tail=None):
    c = bias.shape[1]
    per = c // HY_TILE
    z_col = COL_HY * LANES // HY_TILE
    y, vs = _conv_fwd8(mats, z, z_col + 2 * per, spectrum, 0, n, short=(sw, sb, 2 * per))
    z1 = _conv_inv8(mats_t, y, z, z_col, (sw, sb, 0), vs, bias, 0, n)
    y = _conv_fwd8(mats, z1, 0, spectrum, 1, n)
    return _conv_inv8(mats_t, y, z, z_col + per, (sw, sb, per), z1, bias, 1, n, tail)


def _merge_kernel(x_ref, att_ref, hg_ref, hy_ref, ga_ref, gh_ref, gy_ref, gl_ref, gc_ref, wb_ref, wo_ref,
                  o_ref, *, n_lat, tm):
    def branch(gate_ref, val_ref, k):
        gate = jax.nn.sigmoid(gate_ref[0].astype(F32))
        return gate * jnp.dot(val_ref[0], wb_ref[k], preferred_element_type=F32)

    merged = branch(ga_ref, att_ref, 0) + branch(gh_ref, hg_ref, 1) + branch(gy_ref, hy_ref, 2)
    r = jnp.dot(merged.astype(BF16), wo_ref[...], preferred_element_type=F32)
    row = pl.program_id(1) * tm + lax.broadcasted_iota(jnp.int32, (tm, 1), 0)
    gate = jnp.where(row >= n_lat, gc_ref[0, 0], gl_ref[0, 0])
    o_ref[0] = x_ref[0] + gate * r


def _merge(x, att, hg, hy, z, mod, layer, ctx_row, wb_bf, wo_bf, n_lat, rows):
    bsz, _, d = x.shape
    seq = rows
    tm = _pick(seq, (512, 384, 256))
    tok = lambda: pl.BlockSpec((1, tm, d), lambda b, t: (b, t, 0))
    gate = lambda k: pl.BlockSpec((1, tm, d), lambda b, t: (b, t, COL_GATES * LANES // d + k))
    return pl.pallas_call(
        functools.partial(_merge_kernel, n_lat=n_lat, tm=tm),
        grid=(bsz, seq // tm),
        in_specs=[tok(), tok(), tok(), tok(), gate(0), gate(1), gate(2),
                  pl.BlockSpec((1, 1, 1, d), lambda b, t: (layer, b, 0, 2)),
                  pl.BlockSpec((1, 1, 1, d), lambda b, t: (layer, ctx_row, 0, 2)),
                  pl.BlockSpec((3, d, d), lambda b, t: (0, 0, 0)),
                  pl.BlockSpec((d, d), lambda b, t: (0, 0))],
        out_specs=tok(),
        out_shape=jax.ShapeDtypeStruct((bsz, seq, d), F32),
        compiler_params=pltpu.CompilerParams(dimension_semantics=("parallel", "parallel")),
        name="merge",
    )(x, att, hg, hy, z, z, z, mod, mod, wb_bf, wo_bf)


def _mlp_kernel(x_ref, shl_ref, scl_ref, shc_ref, scc_ref, gl_ref, gc_ref, g_ref, w1_ref, w2_ref, o_ref,
                h_ref, acc_ref, *, n_lat, tm):
    t = pl.program_id(1)
    j = pl.program_id(2)

    @pl.when(j == 0)
    def _():
        h = _modulated_norm(x_ref[0], g_ref[...], shl_ref[0, 0], scl_ref[0, 0], shc_ref[0, 0], scc_ref[0, 0],
                            t * tm, n_lat)
        h_ref[...] = h.astype(BF16)
        acc_ref[...] = jnp.zeros_like(acc_ref)

    a = jnp.maximum(jnp.dot(h_ref[...], w1_ref[...], preferred_element_type=F32), 0.0)
    acc_ref[...] += jnp.dot((a * a).astype(BF16), w2_ref[...], preferred_element_type=F32)

    @pl.when(j == pl.num_programs(2) - 1)
    def _():
        row = t * tm + lax.broadcasted_iota(jnp.int32, (tm, 1), 0)
        gate = jnp.where(row >= n_lat, gc_ref[0, 0], gl_ref[0, 0])
        o_ref[0] = x_ref[0] + gate * acc_ref[...]


def _mlp(x, mod, layer, ctx_row, g, w1_bf, w2_bf, n_lat):
    bsz, seq, d = x.shape
    ff = w1_bf.shape[1]
    tm = _pick(seq, (768, 512, 256))
    tf = _pick(ff, (2048, 1024, 512, 128))
    return pl.pallas_call(
        functools.partial(_mlp_kernel, n_lat=n_lat, tm=tm),
        grid=(bsz, seq // tm, ff // tf),
        in_specs=[pl.BlockSpec((1, tm, d), lambda b, t, j: (b, t, 0))]
        + _mod_specs(layer, ctx_row, 3, 4, d, 3)
        + [pl.BlockSpec((1, 1, 1, d), lambda b, t, j: (layer, b, 0, 5)),
           pl.BlockSpec((1, 1, 1, d), lambda b, t, j: (layer, ctx_row, 0, 5)),
           pl.BlockSpec((1, d), lambda b, t, j: (0, 0)),
           pl.BlockSpec((d, tf), lambda b, t, j: (0, j)),
           pl.BlockSpec((tf, d), lambda b, t, j: (j, 0))],
        out_specs=pl.BlockSpec((1, tm, d), lambda b, t, j: (b, t, 0)),
        out_shape=jax.ShapeDtypeStruct((bsz, seq, d), F32),
        scratch_shapes=[pltpu.VMEM((tm, d), BF16), pltpu.VMEM((tm, d), F32)],
        compiler_params=pltpu.CompilerParams(dimension_semantics=("parallel", "parallel", "arbitrary")),
        name="mlp",
    )(x, mod, mod, mod, mod, mod, mod, g.reshape(1, d), w1_bf, w2_bf)


def kernel(x, c, ctx, c_ctx, ada_w, ada_b, norm1_g, norm2_g, w_in, q_norm_g, k_norm_g, hg_lb_raw, hg_norm_g,
           hy_short_w, hy_short_b, hy_filt_w1, hy_filt_b1, hy_filt_w2, hy_filt_b2, hy_filt_w3, hy_freq, hy_bias,
           w_branch, w_out, w_mlp1, w_mlp2):
    bsz, n_lat, d = x.shape
    n_ctx = ctx.shape[1]
    depth = ada_w.shape[0]
    gate_cols = 3 * d

    cvec = jnp.concatenate([c, c_ctx[None, :]], axis=0)
    ctx_row = bsz
    pad_rows = -cvec.shape[0] % 8
    cvec = jnp.pad(cvec, ((0, pad_rows), (0, 0)))
    mod = _adaln(cvec, ada_w, ada_b)
    mod = mod.reshape(depth, cvec.shape[0], 1, 6 * d)

    cos, sin_signed = _rope_tables(n_lat, n_ctx)
    filt_ctx, dft_ctx = _dft_matrices(n_ctx)
    filt_lat, mats, mats_t = _radix8_matrices(n_lat)

    xs = jnp.concatenate([x, ctx], axis=1)
    for l in range(depth):
        need_ctx = l < depth - 1
        w_l = w_in[l]
        w_bf = jnp.concatenate([w_l[:, -gate_cols:], w_l[:, :-gate_cols]], axis=1).astype(BF16)
        z = _inproj(xs, mod, l, ctx_row, norm1_g[l], w_bf, n_lat)

        att = _attention(z, cos, sin_signed, q_norm_g[l], k_norm_g[l], n_lat, need_ctx)
        hg = _hgrn(z, hg_lb_raw, hg_norm_g[l], l, n_lat, n_ctx, need_ctx)
        w1p = jnp.pad(hy_filt_w1[l], ((0, LANES - HY_EMB), (0, 0)))
        filt_args = (w1p, hy_filt_b1[l], hy_filt_w2[l], hy_filt_b2[l], hy_filt_w3[l], hy_freq[l])
        hy_ctx = None
        if need_ctx:
            hy_ctx = _hyena_dense(z, hy_short_w[l], hy_short_b[l], _hyena_filters(n_ctx, filt_ctx, *filt_args),
                                  dft_ctx, hy_bias[l], n_ctx, n_lat // n_ctx)
        hy = _hyena_radix8(z, hy_short_w[l], hy_short_b[l], _hyena_filters(n_lat, filt_lat, *filt_args),
                           mats, mats_t, hy_bias[l], n_lat, tail=hy_ctx)
        rows = n_lat + n_ctx if need_ctx else n_lat
        xs = _merge(xs, att, hg, hy, z, mod, l, ctx_row, w_branch[l].astype(BF16), w_out[l].astype(BF16), n_lat,
                    rows)
        xs = _mlp(xs, mod, l, ctx_row, norm2_g[l], w_mlp1[l].astype(BF16), w_mlp2[l].astype(BF16), n_lat)
    return xs
```

```python
import functools
import math

import jax
import jax.numpy as jnp
import numpy as np
from jax import lax
from jax.experimental import pallas as pl
from jax.experimental.pallas import tpu as pltpu

F32 = jnp.float32
BF16 = jnp.bfloat16

LANES = 128
HEAD_DIM = 128
ATT_HEADS = 8
ATT_KV_HEADS = 2
ATT_GROUP = ATT_HEADS // ATT_KV_HEADS
ROPE_THETA = 10000.0
GRID_W = 64
HG_HEADS = 8
HG_CHUNK = 64
HG_MIN_F = 1e-6
HG_HEADS_PER_STEP = 4
HG_CHAIN_GROUP = 8
HY_BANDS = 16
HY_EMB = 1 + 2 * HY_BANDS
HY_FAST_DECAY = 0.3
HY_SLOW_DECAY = 1.5
HY_TARGET = 1e-2
HY_MIN_DECAY = math.log(HY_TARGET) / HY_SLOW_DECAY
HY_MAX_DECAY = math.log(HY_TARGET) / HY_FAST_DECAY
EPS = 1e-6
DFT_P = 256
HY_FWD_ROWS = 1024
HY_INV_ROWS = 512
HY_TILE = 512
HY_LANE_GROUP = 256
HIGHEST = lax.Precision.HIGHEST

COL_GATES = 0
COL_ATT_Q = 24
COL_ATT_K = 32
COL_ATT_V = 34
COL_HG_Q = 36
COL_HG_F = 44
COL_HG_B = 52
COL_HG_I = 60
COL_HG_G = 68
COL_HY = 76
N_COLS = 100


def _pick(n, candidates):
    for c in candidates:
        if n % c == 0:
            return c
    raise ValueError(f"no tile in {candidates} divides {n}")


def _nt_dot(a, b):
    return lax.dot_general(a, b, (((1,), (1,)), ((), ())), preferred_element_type=F32)


def _adaln_kernel(c_ref, w_ref, b_ref, o_ref):
    cv = c_ref[...]
    s = cv * jax.nn.sigmoid(cv)
    o_ref[0] = jnp.dot(s, w_ref[0], precision=HIGHEST, preferred_element_type=F32) + b_ref[0]


def _adaln(cvec, ada_w, ada_b):
    depth, d, n6 = ada_w.shape
    rows = cvec.shape[0]
    tn = _pick(n6, (1536, 1024, 512, 128))
    return pl.pallas_call(
        _adaln_kernel,
        grid=(depth, n6 // tn),
        in_specs=[pl.BlockSpec((rows, d), lambda l, j: (0, 0)),
                  pl.BlockSpec((1, d, tn), lambda l, j: (l, 0, j)),
                  pl.BlockSpec((1, 1, tn), lambda l, j: (l, 0, j))],
        out_specs=pl.BlockSpec((1, rows, tn), lambda l, j: (l, 0, j)),
        out_shape=jax.ShapeDtypeStruct((depth, rows, n6), F32),
        name="adaln",
    )(cvec, ada_w, ada_b.reshape(depth, 1, n6))


def _modulated_norm(x, g, sh_l, sc_l, sh_c, sc_c, row0, n_lat):
    y = x * lax.rsqrt(jnp.mean(x * x, axis=-1, keepdims=True) + EPS) * g
    row = row0 + lax.broadcasted_iota(jnp.int32, (x.shape[0], 1), 0)
    is_ctx = row >= n_lat
    sc = jnp.where(is_ctx, sc_c, sc_l)
    sh = jnp.where(is_ctx, sh_c, sh_l)
    return y * (1.0 + sc) + sh


def _mod_specs(layer, ctx_row, k_shift, k_scale, d, nargs):
    def spec(row_fn, k):
        if nargs == 3:
            return pl.BlockSpec((1, 1, 1, d), lambda b, t, j: (layer, row_fn(b), 0, k))
        return pl.BlockSpec((1, 1, 1, d), lambda b, t: (layer, row_fn(b), 0, k))
    lat = lambda b: b
    ctx = lambda b: ctx_row
    return [spec(lat, k_shift), spec(lat, k_scale), spec(ctx, k_shift), spec(ctx, k_scale)]


def _inproj_kernel(x_ref, shl_ref, scl_ref, shc_ref, scc_ref, g_ref, w_ref, o_ref, h_ref, *, n_lat, tm):
    t = pl.program_id(1)

    @pl.when(pl.program_id(2) == 0)
    def _():
        h = _modulated_norm(x_ref[0], g_ref[...], shl_ref[0, 0], scl_ref[0, 0], shc_ref[0, 0], scc_ref[0, 0],
                            t * tm, n_lat)
        h_ref[...] = h.astype(BF16)

    o_ref[0] = jnp.dot(h_ref[...], w_ref[...], preferred_element_type=F32).astype(BF16)


def _inproj(x, mod, layer, ctx_row, g, w_bf, n_lat):
    bsz, seq, d = x.shape
    n = w_bf.shape[1]
    tm = _pick(seq, (768, 512, 256))
    tn = _pick(n, (3200, 2560, 1280, 640, 128))
    return pl.pallas_call(
        functools.partial(_inproj_kernel, n_lat=n_lat, tm=tm),
        grid=(bsz, seq // tm, n // tn),
        in_specs=[pl.BlockSpec((1, tm, d), lambda b, t, j: (b, t, 0))]
        + _mod_specs(layer, ctx_row, 0, 1, d, 3)
        + [pl.BlockSpec((1, d), lambda b, t, j: (0, 0)),
           pl.BlockSpec((d, tn), lambda b, t, j: (0, j))],
        out_specs=pl.BlockSpec((1, tm, tn), lambda b, t, j: (b, t, j)),
        out_shape=jax.ShapeDtypeStruct((bsz, seq, n), BF16),
        scratch_shapes=[pltpu.VMEM((tm, d), BF16)],
        compiler_params=pltpu.CompilerParams(dimension_semantics=("parallel", "parallel", "arbitrary")),
        name="inproj",
    )(x, mod, mod, mod, mod, g.reshape(1, d), w_bf)


def _rope(x, cos, sin_signed):
    lane = lax.broadcasted_iota(jnp.int32, x.shape, 1)
    first = (lane % 64) < 32
    rx = jnp.where(first, pltpu.roll(x, 96, 1), pltpu.roll(x, 32, 1))
    return x * cos + rx * sin_signed


def _head_norm(x, g):
    return x * lax.rsqrt(jnp.mean(x * x, axis=-1, keepdims=True) + EPS) * g


def _attn_kernel(q_ref, k_ref, v_ref, cq_ref, sq_ref, ck_ref, sk_ref, gq_ref, gk_ref, o_ref, ks_ref, vs_ref,
                 *, n_lat, tq):
    qi = pl.program_id(2)

    @pl.when(qi == 0)
    def _():
        k = _head_norm(k_ref[0].astype(F32), gk_ref[...])
        ks_ref[...] = _rope(k, ck_ref[...], sk_ref[...]).astype(BF16)
        vs_ref[:, :HEAD_DIM] = v_ref[0]
        vs_ref[:, HEAD_DIM:] = jnp.ones((vs_ref.shape[0], HEAD_DIM), BF16)

    scale = HEAD_DIM ** -0.5 * math.log2(math.e)

    def attend(keys, vals):
        qall = q_ref[0].astype(F32)
        qs = []
        for g in range(ATT_GROUP):
            qg = _head_norm(qall[:, g * HEAD_DIM:(g + 1) * HEAD_DIM], gq_ref[...])
            qs.append((_rope(qg, cq_ref[...], sq_ref[...]) * scale).astype(BF16))
        ss = [_nt_dot(q, keys) for q in qs]
        ps = [jnp.exp2(s - jnp.max(s, axis=-1, keepdims=True)).astype(BF16) for s in ss]
        for g, p in enumerate(ps):
            o2 = jnp.dot(p, vals, preferred_element_type=F32)
            o = o2[:, :HEAD_DIM] / o2[:, HEAD_DIM:]
            o_ref[0, :, g * HEAD_DIM:(g + 1) * HEAD_DIM] = o.astype(BF16)

    is_lat = qi * tq < n_lat

    @pl.when(is_lat)
    def _():
        attend(ks_ref[...], vs_ref[...])

    @pl.when(jnp.logical_not(is_lat))
    def _():
        attend(ks_ref[n_lat:, :], vs_ref[n_lat:, :])


def _attention(z, cos, sin_signed, gq, gk, n_lat, need_ctx):
    bsz, seq, _ = z.shape
    tq = 256
    n_q = (seq if need_ctx else n_lat) // tq
    gw = ATT_GROUP * HEAD_DIM
    return pl.pallas_call(
        functools.partial(_attn_kernel, n_lat=n_lat, tq=tq),
        grid=(bsz, ATT_KV_HEADS, n_q),
        in_specs=[pl.BlockSpec((1, tq, gw), lambda b, h, i: (b, i, COL_ATT_Q // ATT_GROUP + h)),
                  pl.BlockSpec((1, seq, HEAD_DIM), lambda b, h, i: (b, 0, COL_ATT_K + h)),
                  pl.BlockSpec((1, seq, HEAD_DIM), lambda b, h, i: (b, 0, COL_ATT_V + h)),
                  pl.BlockSpec((tq, HEAD_DIM), lambda b, h, i: (i, 0)),
                  pl.BlockSpec((tq, HEAD_DIM), lambda b, h, i: (i, 0)),
                  pl.BlockSpec((seq, HEAD_DIM), lambda b, h, i: (0, 0)),
                  pl.BlockSpec((seq, HEAD_DIM), lambda b, h, i: (0, 0)),
                  pl.BlockSpec((1, HEAD_DIM), lambda b, h, i: (0, 0)),
                  pl.BlockSpec((1, HEAD_DIM), lambda b, h, i: (0, 0))],
        out_specs=pl.BlockSpec((1, tq, gw), lambda b, h, i: (b, i, h)),
        out_shape=jax.ShapeDtypeStruct((bsz, n_q * tq, ATT_HEADS * HEAD_DIM), BF16),
        scratch_shapes=[pltpu.VMEM((seq, HEAD_DIM), BF16), pltpu.VMEM((seq, 2 * HEAD_DIM), BF16)],
        compiler_params=pltpu.CompilerParams(dimension_semantics=("parallel", "parallel", "arbitrary")),
        name="attention",
    )(z, z, z, cos, sin_signed, cos, sin_signed, gq.reshape(1, HEAD_DIM), gk.reshape(1, HEAD_DIM))


def _rope_tables(n_lat, n_ctx):
    half = HEAD_DIM // 2
    inv = ROPE_THETA ** (-np.arange(0, half, 2, dtype=np.float64) / half)
    t = np.arange(n_lat)
    ar = (t // GRID_W)[:, None] * inv
    ac = (t % GRID_W)[:, None] * inv
    ang = np.concatenate([ar, ar, ac, ac], axis=-1)
    cos = np.concatenate([np.cos(ang), np.ones((n_ctx, HEAD_DIM))], axis=0)
    sin = np.concatenate([np.sin(ang), np.zeros((n_ctx, HEAD_DIM))], axis=0)
    sign = np.where((np.arange(HEAD_DIM) % 64) < 32, -1.0, 1.0)
    return jnp.asarray(cos, F32), jnp.asarray(sin * sign, F32)


def _hgrn_masks():
    c = HG_CHUNK
    idx = np.arange(c)
    t, s = idx[:, None], idx[None, :]

    def halving(size):
        half = size // 2
        return (t // size == s // size) & (t % size >= half) & (s % size < half)

    diag = (t // 8 == s // 8) & (s <= t)
    flip = lambda m: m[::-1, ::-1]
    fwd = np.stack([np.concatenate([halving(64), halving(32)], axis=1),
                    np.concatenate([halving(16), diag], axis=1)])
    bwd = np.stack([np.concatenate([flip(halving(64)), flip(halving(32))], axis=1),
                    np.concatenate([flip(halving(16)), flip(diag)], axis=1)])
    return jnp.asarray(np.stack([fwd, bwd]), F32)


def _hgrn_exponents(zg, lb, reverse):
    tile = 8
    nt = HG_CHUNK // tile
    sig = jax.nn.sigmoid(zg.astype(F32))
    f = sig if lb is None else lb + (1.0 - lb) * sig
    lf = jnp.log2(jnp.maximum(f, HG_MIN_F))
    kk = 1.0 - f
    p = lf.reshape(nt, tile, LANES)
    sub = lax.broadcasted_iota(jnp.int32, p.shape, 1)
    for k in (1, 2, 4):
        if reverse:
            p = p + jnp.where(sub < tile - k, pltpu.roll(p, tile - k, 1), 0.0)
        else:
            p = p + jnp.where(sub >= k, pltpu.roll(p, k, 1), 0.0)
    order = list(range(nt - 1, -1, -1)) if reverse else list(range(nt))
    tot_row, mid_row = (0, 4) if reverse else (tile - 1, 3)
    s = [p[j] for j in order]
    tb = [jnp.broadcast_to(p[j, tot_row:tot_row + 1, :], (tile, LANES)) for j in order]
    mb = [jnp.broadcast_to(p[j, mid_row:mid_row + 1, :], (tile, LANES)) for j in order]
    t01, t23, t45 = tb[0] + tb[1], tb[2] + tb[3], tb[4] + tb[5]
    t123, t456 = tb[1] + t23, t45 + tb[6]
    c4 = t01 + t23
    before = [None, tb[0], t01, t01 + tb[2], c4, c4 + tb[4], c4 + t45, c4 + t456]
    total = before[7] + tb[7]
    cum = [s[0]] + [before[k] + s[k] for k in range(1, nt)]
    suf = [total - cum[k] for k in range(nt)]
    w8 = [s[k] - mb[k] for k in range(nt)]
    w16 = [s[k] if k % 2 else tb[k] - s[k] for k in range(nt)]
    w32 = [t01 - s[0], tb[1] - s[1], s[2], tb[2] + s[3], t45 - s[4], tb[5] - s[5], s[6], tb[6] + s[7]]
    w64 = [c4 - s[0], t123 - s[1], t23 - s[2], tb[3] - s[3], s[4], tb[4] + s[5], t45 + s[6], t456 + s[7]]

    def natural(parts):
        return jnp.concatenate([parts[order.index(j)] for j in range(nt)], axis=0)

    ex = {"cum": natural(cum), "suf": natural(suf), 64: natural(w64), 32: natural(w32), 16: natural(w16),
          8: natural(w8), "tot": total[0:1]}
    return kk, ex


def _hgrn_scores(q, kk, ex, st):
    qf = q.astype(F32)
    inter = _nt_dot((qf * jnp.exp2(ex["cum"])).astype(BF16), st.astype(BF16))
    zeros = jnp.zeros(q.shape, BF16)
    pairs = []
    for la, lb_ in ((64, 32), (16, 8)):
        ea, eb = jnp.exp2(ex[la]), jnp.exp2(ex[lb_])
        eb_k = jnp.exp2(-ex[lb_]) if lb_ == 8 else eb
        qa = jnp.concatenate([(qf * ea).astype(BF16), (qf * eb).astype(BF16)], axis=1)
        ka = jnp.concatenate([(kk * ea).astype(BF16), zeros], axis=1)
        kb = jnp.concatenate([zeros, (kk * eb_k).astype(BF16)], axis=1)
        pairs.append(_nt_dot(qa, jnp.concatenate([ka, kb], axis=0)))
    return inter, pairs


def _hgrn_output(inter, pairs, v, mask):
    p0 = pairs[0] * mask[0]
    p1 = jnp.where(mask[1] > 0.0, pairs[1], 0.0)
    probs = jnp.concatenate([p0.astype(BF16), p1.astype(BF16)], axis=1)
    return inter + jnp.dot(probs, jnp.concatenate([v, v, v, v], axis=0), preferred_element_type=F32)


def _hgrn_state(kk, ex, v, st):
    k_suf = (kk * jnp.exp2(ex["suf"])).astype(BF16)
    upd = lax.dot_general(v, k_suf, (((0,), (0,)), ((), ())), preferred_element_type=F32)
    return st * jnp.exp2(ex["tot"]) + upd


def _hgrn_kernel(q_ref, zf_ref, zb_ref, i_ref, g_ref, lbraw_ref, gn_ref, mask_ref, o_ref,
                 acc_ref, st_ref, *, layer, n_lat, n_ctx, need_ctx):
    c = HG_CHUNK
    nc_lat, nc_ctx = n_lat // c, n_ctx // c
    nc = nc_lat + nc_ctx
    raw = lbraw_ref[...].astype(F32)
    ew = jnp.exp(raw - jnp.max(raw, axis=0, keepdims=True))
    sm = ew / jnp.sum(ew, axis=0, keepdims=True)
    lower = jnp.zeros_like(sm[0])
    for dpt in range(1, layer + 1):
        lower = lower + sm[dpt]

    acc_ref[...] = jnp.zeros_like(acc_ref)
    st_ref[...] = jnp.zeros_like(st_ref)

    def step(n, carry, with_outputs=True):
        cf = jnp.where(n < nc_ctx, nc_lat + n, n - nc_ctx)
        cb = nc - 1 - n
        chains = []
        for hh in range(HG_HEADS_PER_STEP):
            lanes = slice(hh * LANES, (hh + 1) * LANES)
            for d, (chunk, z_ref) in enumerate(((cf, zf_ref), (cb, zb_ref))):
                rows = pl.ds(pl.multiple_of(chunk * c, c), c)
                chains.append((hh, d, rows, lanes, z_ref))
        for g0 in range(0, len(chains), HG_CHAIN_GROUP):
            group = chains[g0:g0 + HG_CHAIN_GROUP]
            gates = [_hgrn_exponents(z_ref[0, rows, lanes], None if layer == 0 else lower[hh, d:d + 1], d == 1)
                     for hh, d, rows, lanes, z_ref in group]
            if with_outputs:
                scored = [_hgrn_scores(q_ref[0, rows, lanes], kk, ex, st_ref[hh, d])
                          for (hh, d, rows, lanes, _), (kk, ex) in zip(group, gates)]
                for (hh, d, rows, lanes, _), (inter, pairs) in zip(group, scored):
                    out = _hgrn_output(inter, pairs, i_ref[0, rows, lanes], mask_ref[d])
                    acc_ref[rows, lanes] = acc_ref[rows, lanes] + out
            for (hh, d, rows, lanes, _), (kk, ex) in zip(group, gates):
                st_ref[hh, d] = _hgrn_state(kk, ex, i_ref[0, rows, lanes], st_ref[hh, d])
        return carry

    if need_ctx:
        lax.fori_loop(0, nc, step, 0)
    else:
        lax.fori_loop(0, nc_ctx, functools.partial(step, with_outputs=False), 0)
        lax.fori_loop(nc_ctx, nc, step, 0)

    rows_out = o_ref.shape[1]
    for hh in range(HG_HEADS_PER_STEP):
        lanes = slice(hh * LANES, (hh + 1) * LANES)
        o = acc_ref[:, lanes]
        y = o * lax.rsqrt(jnp.mean(o * o, axis=-1, keepdims=True) + EPS) * gn_ref[...]
        g = g_ref[0, :rows_out, lanes].astype(F32)
        o_ref[0, :, lanes] = (y * (g * jax.nn.sigmoid(g))).astype(BF16)


def _hgrn(z, lb_raw, g_norm, layer, n_lat, n_ctx, need_ctx):
    bsz, seq, _ = z.shape
    depth = lb_raw.shape[0]
    hps = HG_HEADS_PER_STEP
    width = hps * LANES
    masks = _hgrn_masks()
    rows_out = seq if need_ctx else n_lat
    lbr = lb_raw.reshape(depth, 2, HG_HEADS, LANES).transpose(0, 2, 1, 3)
    col = lambda c0: pl.BlockSpec((1, seq, width), lambda b, h: (b, 0, c0 // hps + h))
    return pl.pallas_call(
        functools.partial(_hgrn_kernel, layer=layer, n_lat=n_lat, n_ctx=n_ctx, need_ctx=need_ctx),
        grid=(bsz, HG_HEADS // hps),
        in_specs=[col(COL_HG_Q), col(COL_HG_F), col(COL_HG_B), col(COL_HG_I), col(COL_HG_G),
                  pl.BlockSpec((depth, hps, 2, LANES), lambda b, h: (0, h, 0, 0)),
                  pl.BlockSpec((1, LANES), lambda b, h: (0, 0)),
                  pl.BlockSpec((2, 2, HG_CHUNK, LANES), lambda b, h: (0, 0, 0, 0))],
        out_specs=pl.BlockSpec((1, rows_out, width), lambda b, h: (b, 0, h)),
        out_shape=jax.ShapeDtypeStruct((bsz, rows_out, HG_HEADS * LANES), BF16),
        scratch_shapes=[pltpu.VMEM((rows_out, width), F32), pltpu.VMEM((hps, 2, LANES, LANES), F32)],
        compiler_params=pltpu.CompilerParams(dimension_semantics=("parallel", "parallel")),
        name="hgrn",
    )(z, z, z, z, z, lbr, g_norm.reshape(1, LANES), masks)


def _dft_matrices(n):
    size = 2 * n
    k = np.arange(n)[:, None]
    t = np.arange(n)[None, :]
    ang = (2.0 * np.pi / size) * ((k * t) % size)
    cos, sin = np.cos(ang), np.sin(ang)
    sin[0, :] = np.where(np.arange(n) % 2 == 0, 1.0, -1.0)
    fwd = np.concatenate([cos.reshape(n // DFT_P, DFT_P, n), sin.reshape(n // DFT_P, DFT_P, n)], axis=1)
    fwd = fwd.reshape(size, n)
    pairs = [(cos[i:i + DFT_P], sin[i:i + DFT_P]) for i in range(0, n, DFT_P)]
    filt = _group_filter_rows(pairs, _filter_group(n))
    return jnp.asarray(filt, BF16), (jnp.asarray(fwd, BF16), jnp.asarray(fwd.T.copy(), BF16))


def _hyena_positions(n):
    t = np.linspace(0.0, 1.0, n)[:, None]
    w = (2.0 * math.pi / n) * np.arange(n)[:, None]
    f = np.linspace(1e-4, HY_BANDS - 1, HY_BANDS)[None, :]
    z = np.concatenate([t, np.cos(f * w), -np.sin(f * w)], axis=-1)
    return jnp.asarray(np.pad(z, ((0, 0), (0, LANES - HY_EMB))), F32)


def _filter_kernel(pos_ref, w1_ref, b1_ref, w2_ref, b2_ref, fr_ref, w3f_ref, w3b_ref, dl_ref, fs_ref, fd_ref,
                   h_ref):
    pos = pos_ref[...]

    @pl.when((pl.program_id(0) == 0) & (pl.program_id(1) == 0))
    def _():
        h1 = jnp.sin(fr_ref[0:1] * (jnp.dot(pos, w1_ref[...], precision=HIGHEST, preferred_element_type=F32)
                                    + b1_ref[...]))
        h_ref[...] = jnp.sin(fr_ref[1:2] * (jnp.dot(h1, w2_ref[...], precision=HIGHEST,
                                                    preferred_element_type=F32) + b2_ref[...]))

    h = h_ref[...]
    window = jnp.exp(-pos[:, 0:1] * dl_ref[...])
    h_hi = h.astype(BF16)
    h_lo = (h - h_hi.astype(F32)).astype(BF16)

    def project(w):
        w_hi = w.astype(BF16)
        w_lo = (w - w_hi.astype(F32)).astype(BF16)
        return (jnp.dot(h_hi, w_hi, preferred_element_type=F32) + jnp.dot(h_lo, w_hi, preferred_element_type=F32)
                + jnp.dot(h_hi, w_lo, preferred_element_type=F32))

    hf = project(w3f_ref[...]) * window
    hb = project(w3b_ref[...]) * window
    row = lax.broadcasted_iota(jnp.int32, (pos.shape[0], 1), 0)
    hb = jnp.where(row == 0, 0.0, hb)
    norm = jnp.sum(jnp.abs(hf), axis=0, keepdims=True) + jnp.sum(jnp.abs(hb), axis=0, keepdims=True) + EPS
    inv = 1.0 / norm
    fs_ref[0] = ((hf + hb) * inv).astype(BF16)
    fd_ref[0] = ((hf - hb) * inv).astype(BF16)


def _filter_group(n):
    chunks = n // DFT_P
    return 4 if chunks % 4 == 0 else 1


def _group_filter_rows(pairs, group):
    out = []
    for g0 in range(0, len(pairs), group):
        out += [c for c, _ in pairs[g0:g0 + group]] + [s for _, s in pairs[g0:g0 + group]]
    return np.concatenate(out)


def _filter_dft_kernel(a_ref, fs_ref, fd_ref, o_ref, *, n, group):
    m = pl.program_id(2)
    p = DFT_P
    gp = group * p
    first = (lax.broadcasted_iota(jnp.int32, (gp, 1), 0) == 0) & (m == 0)
    weight = jnp.where(first, 1.0, 2.0) / (2 * n)
    hc = jnp.dot(a_ref[:gp], fs_ref[0], preferred_element_type=F32) * weight
    hs = jnp.dot(a_ref[gp:], fd_ref[0], preferred_element_type=F32) * weight
    for g in range(group):
        o_ref[0, 2 * p * g:2 * p * g + p] = hc[g * p:(g + 1) * p]
        o_ref[0, 2 * p * g + p:2 * p * (g + 1)] = hs[g * p:(g + 1) * p]

    @pl.when(m == 0)
    def _():
        hny = jnp.dot(a_ref[gp:gp + 16], fs_ref[0], preferred_element_type=F32)[:8]
        o_ref[0, p:p + 8] = jnp.where(first[:8], hny / (2 * n), hs[:8])


def _hyena_filters(n, fwd, w1p, b1, w2, b2, w3, freq):
    hid = w2.shape[0]
    c = w3.shape[1] // 4
    tn = _pick(c, (512, 256, 128))
    nct = c // tn
    pos = _hyena_positions(n)
    deltas = jnp.asarray(np.abs(np.linspace(HY_MIN_DECAY, HY_MAX_DECAY, c))[None, :], F32)
    full = lambda shape: pl.BlockSpec(shape, lambda o, j: (0,) * len(shape))
    fsum, fdiff = pl.pallas_call(
        _filter_kernel,
        grid=(2, nct),
        in_specs=[full((n, LANES)), full((LANES, hid)), full((1, hid)), full((hid, hid)), full((1, hid)),
                  full((2, hid)),
                  pl.BlockSpec((hid, tn), lambda o, j: (0, (2 * o) * nct + j)),
                  pl.BlockSpec((hid, tn), lambda o, j: (0, (2 * o + 1) * nct + j)),
                  pl.BlockSpec((1, tn), lambda o, j: (0, j))],
        out_specs=[pl.BlockSpec((1, n, tn), lambda o, j: (o, 0, j))] * 2,
        out_shape=[jax.ShapeDtypeStruct((2, n, c), BF16)] * 2,
        scratch_shapes=[pltpu.VMEM((n, hid), F32)],
        compiler_params=pltpu.CompilerParams(dimension_semantics=("arbitrary", "arbitrary")),
        name="hyena_filter",
    )(pos, w1p, b1.reshape(1, hid), w2, b2.reshape(1, hid), freq, w3, w3, deltas)
    group = _filter_group(n)
    rows = 2 * DFT_P * group
    return pl.pallas_call(
        functools.partial(_filter_dft_kernel, n=n, group=group),
        grid=(2, nct, 2 * n // rows),
        in_specs=[pl.BlockSpec((rows, n), lambda o, j, m: (m, 0)),
                  pl.BlockSpec((1, n, tn), lambda o, j, m: (o, 0, j)),
                  pl.BlockSpec((1, n, tn), lambda o, j, m: (o, 0, j))],
        out_specs=pl.BlockSpec((1, rows, tn), lambda o, j, m: (o, m, j)),
        out_shape=jax.ShapeDtypeStruct((2, 2 * n, c), F32),
        name="hyena_filter_dft",
    )(fwd, fsum, fdiff)


def _short_conv(u, w, b):
    n = u.shape[0]
    row = lax.broadcasted_iota(jnp.int32, (n, 1), 0)
    prev = jnp.where(row == 0, 0.0, pltpu.roll(u, 1, 0))
    nxt = jnp.where(row == n - 1, 0.0, pltpu.roll(u, n - 1, 0))
    return prev * w[0:1] + u * w[1:2] + nxt * w[2:3] + b


def _shortconv_kernel(u_ref, w_ref, b_ref, o_ref):
    o_ref[0] = _short_conv(u_ref[0].astype(F32), w_ref[...], b_ref[...]).astype(BF16)


def _shortconv(z, w, b, n, row_blk):
    bsz = z.shape[0]
    width = w.shape[1]
    tn = HY_TILE
    assert (COL_HY * LANES) % tn == 0 and width % tn == 0
    c0 = COL_HY * LANES // tn
    return pl.pallas_call(
        _shortconv_kernel,
        grid=(bsz, width // tn),
        in_specs=[pl.BlockSpec((1, n, tn), lambda bb, j: (bb, row_blk, c0 + j)),
                  pl.BlockSpec((3, tn), lambda bb, j: (0, j)),
                  pl.BlockSpec((1, tn), lambda bb, j: (0, j))],
        out_specs=pl.BlockSpec((1, n, tn), lambda bb, j: (bb, 0, j)),
        out_shape=jax.ShapeDtypeStruct((bsz, n, width), BF16),
        compiler_params=pltpu.CompilerParams(dimension_semantics=("parallel", "parallel")),
        name="hyena_shortconv",
    )(z, w, b.reshape(1, width))


def _conv_fwd_kernel(a_ref, v_ref, h_ref, y_ref, *, chunks):
    p = DFT_P
    zf = jnp.dot(a_ref[...], v_ref[0], preferred_element_type=F32)
    for k in range(chunks):
        lo = 2 * p * k
        zc, zs = zf[lo:lo + p], zf[lo + p:lo + 2 * p]
        hc, hs = h_ref[0, lo:lo + p], h_ref[0, lo + p:lo + 2 * p]
        yc, ys = zc * hc - zs * hs, zc * hs + zs * hc
        if k == 0:
            real_row = (lax.broadcasted_iota(jnp.int32, (p, 1), 0) == 0) & (pl.program_id(1) == 0)
            yc = jnp.where(real_row, zc * hc, yc)
            ys = jnp.where(real_row, zs * hs, ys)
        y_ref[0, lo:lo + p] = yc.astype(BF16)
        y_ref[0, lo + p:lo + 2 * p] = ys.astype(BF16)


def _conv_fwd(fwd, v_arr, v_col, spec, order, n, row_blk):
    bsz = v_arr.shape[0]
    c = spec.shape[2]
    rows = min(HY_FWD_ROWS, 2 * n)
    return pl.pallas_call(
        functools.partial(_conv_fwd_kernel, chunks=rows // (2 * DFT_P)),
        grid=(bsz, 2 * n // rows),
        in_specs=[pl.BlockSpec((rows, n), lambda b, m: (m, 0)),
                  pl.BlockSpec((1, n, c), lambda b, m: (b, row_blk, v_col)),
                  pl.BlockSpec((1, rows, c), lambda b, m: (order, m, 0))],
        out_specs=pl.BlockSpec((1, rows, c), lambda b, m: (b, m, 0)),
        out_shape=jax.ShapeDtypeStruct((bsz, 2 * n, c), BF16),
        compiler_params=pltpu.CompilerParams(dimension_semantics=("parallel", "arbitrary")),
        name="hyena_conv_fwd",
    )(fwd, v_arr, spec)


def _conv_inv_kernel(g_ref, y_ref, gate_ref, zin_ref, bias_ref, o_ref):
    y = jnp.dot(g_ref[...], y_ref[0], preferred_element_type=F32)
    o_ref[0] = (gate_ref[0].astype(F32) * (y + zin_ref[0].astype(F32) * bias_ref[0])).astype(BF16)


def _conv_inv(inv, y, gate, zin, bias, order, n):
    bsz, _, c = y.shape
    tm = min(HY_INV_ROWS, n)

    def rows_of(triple):
        _, row0, col = triple
        return pl.BlockSpec((1, tm, c), lambda b, m: (b, row0 // tm + m, col))

    return pl.pallas_call(
        _conv_inv_kernel,
        grid=(bsz, n // tm),
        in_specs=[pl.BlockSpec((tm, 2 * n), lambda b, m: (m, 0)),
                  pl.BlockSpec((1, 2 * n, c), lambda b, m: (b, 0, 0)),
                  rows_of(gate), rows_of(zin),
                  pl.BlockSpec((1, 1, c), lambda b, m: (order, 0, 0))],
        out_specs=pl.BlockSpec((1, tm, c), lambda b, m: (b, m, 0)),
        out_shape=jax.ShapeDtypeStruct((bsz, n, c), BF16),
        compiler_params=pltpu.CompilerParams(dimension_semantics=("parallel", "arbitrary")),
        name="hyena_conv_inv",
    )(inv, y, gate[0], zin[0], bias.reshape(2, 1, c))


def _hyena_dense(z, sw, sb, spectrum, dft, bias, n, row_blk):
    u = _shortconv(z, sw, sb, n, row_blk)
    x1_col, x2_col, v_col = 0, 1, 2
    fwd, inv = dft
    y = _conv_fwd(fwd, u, v_col, spectrum, 0, n, 0)
    z1 = _conv_inv(inv, y, (u, 0, x1_col), (u, 0, v_col), bias, 0, n)
    y = _conv_fwd(fwd, z1, 0, spectrum, 1, n, 0)
    return _conv_inv(inv, y, (u, 0, x2_col), (z1, 0, 0), bias, 1, n)


def _radix8_frequencies(n):
    p = n // 8
    j = np.arange(p)
    return [8 * j, 4 + 8 * j, 1 + 8 * j, 1 + 8 * (j + p), 2 + 8 * j, 2 + 8 * (j + p), 3 + 8 * j, 3 + 8 * (j + p)]


def _radix8_matrices(n):
    size, p = 2 * n, n // 8
    freqs = _radix8_frequencies(n)

    def cos_sin(k, length):
        ang = (2.0 * np.pi / size) * ((k[:, None] * np.arange(length)[None, :]) % size)
        return np.cos(ang), np.sin(ang)

    def nyquist(sin_rows, length):
        sin_rows[0, :] = np.where(np.arange(length) % 2 == 0, 1.0, -1.0)
        return sin_rows

    pairs = []
    for idx, k in enumerate(freqs):
        c, s = cos_sin(k, n)
        pairs.append((c, nyquist(s, n) if idx == 0 else s))
    filt = _group_filter_rows(pairs, _filter_group(n))
    slab = size // 8
    c0, s0 = cos_sin(freqs[0], slab)
    c4, s4 = cos_sin(freqs[1], slab)
    mats = [np.concatenate([c0, nyquist(s0, slab)]), np.concatenate([c4, s4])]
    for k1 in (1, 2, 3):
        ca, sa = cos_sin(freqs[2 * k1], slab)
        cb, sb = cos_sin(freqs[2 * k1 + 1], slab)
        mats.append(np.block([[ca, sa], [sa, -ca], [cb, sb], [sb, -cb]]))
    to_bf = lambda a: jnp.asarray(a, BF16)
    return to_bf(filt), [to_bf(m) for m in mats], [to_bf(m.T.copy()) for m in mats]


def _conv_fwd8_kernel(*refs, pre_conv):
    if pre_conv:
        v_ref, w_ref, b_ref, r0_ref, r4_ref, r1_ref, r2_ref, r3_ref, h_ref, y_ref, vs_ref = refs
    else:
        v_ref, r0_ref, r4_ref, r1_ref, r2_ref, r3_ref, h_ref, y_ref = refs
    n, tn = v_ref.shape[1], v_ref.shape[2]
    slab, p = n // 4, n // 8
    bf = lambda x: x.astype(BF16)
    cat = lambda re, im: jnp.concatenate([bf(re), bf(im)], axis=0)
    groups = [slice(g * HY_LANE_GROUP, (g + 1) * HY_LANE_GROUP) for g in range(tn // HY_LANE_GROUP)]
    inputs = []
    for lanes in groups:
        v = v_ref[0, :, lanes].astype(F32)
        if pre_conv:
            v = _short_conv(v, w_ref[:, lanes], b_ref[:, lanes])
            vs_ref[0, :, lanes] = v.astype(BF16)
        z0, z1, z2, z3 = (v[i * slab:(i + 1) * slab] for i in range(4))
        e, o = z0 + z2, z1 + z3
        a, b = (z1 - z3) * (0.5 ** 0.5), o * (0.5 ** 0.5)
        inputs.append((bf(e + o), bf(e - o), cat(z0 + a, -z2 - b), cat(z0 - z2, z3 - z1), cat(z0 - a, z2 - b)))
    spectra = [[jnp.dot(r[...], x, preferred_element_type=F32)
                for r, x in zip((r0_ref, r4_ref, r1_ref, r2_ref, r3_ref), xs)] for xs in inputs]
    for lanes, spec in zip(groups, spectra):
        chunks = [spec[0], spec[1]] + [s[i * 2 * p:(i + 1) * 2 * p] for s in spec[2:] for i in range(2)]
        for c, zf in enumerate(chunks):
            lo = 2 * p * c
            zc, zs = zf[:p], zf[p:]
            hc, hs = h_ref[0, lo:lo + p, lanes], h_ref[0, lo + p:lo + 2 * p, lanes]
            yc, ys = zc * hc - zs * hs, zc * hs + zs * hc
            if c == 0:
                real_row = lax.broadcasted_iota(jnp.int32, (p, 1), 0) == 0
                yc = jnp.where(real_row, zc * hc, yc)
                ys = jnp.where(real_row, zs * hs, ys)
            y_ref[0, lo:lo + p, lanes] = yc.astype(BF16)
            y_ref[0, lo + p:lo + 2 * p, lanes] = ys.astype(BF16)


def _const_spec(shape):
    return pl.BlockSpec(shape, lambda j, b: (0,) * len(shape), pipeline_mode=pl.Buffered(1))


def _conv_fwd8(mats, v_arr, v_col, spec, order, n, short=None):
    bsz = v_arr.shape[0]
    c = spec.shape[2]
    tn = HY_TILE
    in_specs = [pl.BlockSpec((1, n, tn), lambda j, b: (b, 0, v_col + j))]
    args = [v_arr]
    out_specs = [pl.BlockSpec((1, 2 * n, tn), lambda j, b: (b, 0, j))]
    out_shape = [jax.ShapeDtypeStruct((bsz, 2 * n, c), BF16)]
    if short is not None:
        w, bvec, wcol = short
        in_specs += [pl.BlockSpec((3, tn), lambda j, b: (0, wcol + j)),
                     pl.BlockSpec((1, tn), lambda j, b: (0, wcol + j))]
        args += [w, bvec.reshape(1, -1)]
        out_specs.append(pl.BlockSpec((1, n, tn), lambda j, b: (b, 0, j)))
        out_shape.append(jax.ShapeDtypeStruct((bsz, n, c), BF16))
    in_specs += [_const_spec(m.shape) for m in mats]
    in_specs.append(pl.BlockSpec((1, 2 * n, tn), lambda j, b: (order, 0, j)))
    res = pl.pallas_call(
        functools.partial(_conv_fwd8_kernel, pre_conv=short is not None),
        grid=(c // tn, bsz),
        in_specs=in_specs,
        out_specs=out_specs,
        out_shape=out_shape,
        compiler_params=pltpu.CompilerParams(dimension_semantics=("parallel", "parallel")),
        name="hyena_conv_fwd8",
    )(*args, *mats, spec)
    return res if short is not None else res[0]


def _conv_inv8_kernel(y_ref, r0_ref, r4_ref, r1_ref, r2_ref, r3_ref, gate_ref, w_ref, b_ref, zin_ref, bias_ref,
                      *rest):
    o_ref = rest[-1]
    n, tn = gate_ref.shape[1], gate_ref.shape[2]
    if len(rest) == 2:
        o_ref[0, n:, :] = rest[0][0]
    slab = n // 4
    groups = [slice(g * HY_LANE_GROUP, (g + 1) * HY_LANE_GROUP) for g in range(tn // HY_LANE_GROUP)]
    bounds = (0, slab, 2 * slab, 4 * slab, 6 * slab, 8 * slab)
    gates = [_short_conv(gate_ref[0, :, lanes].astype(F32), w_ref[:, lanes], b_ref[:, lanes]) for lanes in groups]
    skips = [zin_ref[0, :, lanes].astype(F32) * bias_ref[0, :, lanes] for lanes in groups]
    parts = [[jnp.dot(r[...], y_ref[0, lo:hi, lanes], preferred_element_type=F32)
              for r, lo, hi in zip((r0_ref, r4_ref, r1_ref, r2_ref, r3_ref), bounds[:-1], bounds[1:])]
             for lanes in groups]
    for lanes, gate, skip, (v0, v4, v1, v2, v3) in zip(groups, gates, skips, parts):
        (v1r, v1i), (v2r, v2i), (v3r, v3i) = ((v[:slab], v[slab:]) for v in (v1, v2, v3))
        s, d = v0 + v4, v0 - v4
        r = 0.5 ** 0.5
        outs = [s + v1r + v2r + v3r,
                d + (v1r - v1i - v3r - v3i) * r - v2i,
                s - v1i - v2r + v3i,
                d + (v3r - v3i - v1r - v1i) * r + v2i]
        for i, y in enumerate(outs):
            rows = slice(i * slab, (i + 1) * slab)
            o_ref[0, rows, lanes] = (gate[rows] * (y + skip[rows])).astype(BF16)


def _conv_inv8(mats_t, y, gate_arr, gate_col, short, zin, bias, order, n, tail=None):
    bsz, _, c = y.shape
    tn = HY_TILE
    w, bvec, wcol = short
    rows_out = n if tail is None else n + tail.shape[1]
    tail_specs = [] if tail is None else [pl.BlockSpec((1, tail.shape[1], tn), lambda j, b: (b, 0, j))]
    tail_args = [] if tail is None else [tail]
    return pl.pallas_call(
        _conv_inv8_kernel,
        grid=(c // tn, bsz),
        in_specs=[pl.BlockSpec((1, 2 * n, tn), lambda j, b: (b, 0, j))]
        + [_const_spec(m.shape) for m in mats_t]
        + [pl.BlockSpec((1, n, tn), lambda j, b: (b, 0, gate_col + j)),
           pl.BlockSpec((3, tn), lambda j, b: (0, wcol + j)),
           pl.BlockSpec((1, tn), lambda j, b: (0, wcol + j)),
           pl.BlockSpec((1, n, tn), lambda j, b: (b, 0, j)),
           pl.BlockSpec((1, 1, tn), lambda j, b: (order, 0, j))] + tail_specs,
        out_specs=pl.BlockSpec((1, rows_out, tn), lambda j, b: (b, 0, j)),
        out_shape=jax.ShapeDtypeStruct((bsz, rows_out, c), BF16),
        compiler_params=pltpu.CompilerParams(dimension_semantics=("parallel", "parallel")),
        name="hyena_conv_inv8",
    )(y, *mats_t, gate_arr, w, bvec.reshape(1, -1), zin, bias.reshape(2, 1, c), *tail_args)


def _hyena_radix8(z, sw, sb, spectrum, mats, mats_t, bias, n, tail=None):
    c = bias.shape[1]
    per = c // HY_TILE
    z_col = COL_HY * LANES // HY_TILE
    y, vs = _conv_fwd8(mats, z, z_col + 2 * per, spectrum, 0, n, short=(sw, sb, 2 * per))
    z1 = _conv_inv8(mats_t, y, z, z_col, (sw, sb, 0), vs, bias, 0, n)
    y = _conv_fwd8(mats, z1, 0, spectrum, 1, n)
    return _conv_inv8(mats_t, y, z, z_col + per, (sw, sb, per), z1, bias, 1, n, tail)


def _merge_kernel(x_ref, att_ref, hg_ref, hy_ref, ga_ref, gh_ref, gy_ref, gl_ref, gc_ref, wb_ref, wo_ref,
                  o_ref, *, n_lat, tm):
    def branch(gate_ref, val_ref, k):
        gate = jax.nn.sigmoid(gate_ref[0].astype(F32))
        return gate * jnp.dot(val_ref[0], wb_ref[k], preferred_element_type=F32)

    merged = branch(ga_ref, att_ref, 0) + branch(gh_ref, hg_ref, 1) + branch(gy_ref, hy_ref, 2)
    r = jnp.dot(merged.astype(BF16), wo_ref[...], preferred_element_type=F32)
    row = pl.program_id(1) * tm + lax.broadcasted_iota(jnp.int32, (tm, 1), 0)
    gate = jnp.where(row >= n_lat, gc_ref[0, 0], gl_ref[0, 0])
    o_ref[0] = x_ref[0] + gate * r


def _merge(x, att, hg, hy, z, mod, layer, ctx_row, wb_bf, wo_bf, n_lat, rows):
    bsz, _, d = x.shape
    seq = rows
    tm = _pick(seq, (512, 384, 256))
    tok = lambda: pl.BlockSpec((1, tm, d), lambda b, t: (b, t, 0))
    gate = lambda k: pl.BlockSpec((1, tm, d), lambda b, t: (b, t, COL_GATES * LANES // d + k))
    return pl.pallas_call(
        functools.partial(_merge_kernel, n_lat=n_lat, tm=tm),
        grid=(bsz, seq // tm),
        in_specs=[tok(), tok(), tok(), tok(), gate(0), gate(1), gate(2),
                  pl.BlockSpec((1, 1, 1, d), lambda b, t: (layer, b, 0, 2)),
                  pl.BlockSpec((1, 1, 1, d), lambda b, t: (layer, ctx_row, 0, 2)),
                  pl.BlockSpec((3, d, d), lambda b, t: (0, 0, 0)),
                  pl.BlockSpec((d, d), lambda b, t: (0, 0))],
        out_specs=tok(),
        out_shape=jax.ShapeDtypeStruct((bsz, seq, d), F32),
        compiler_params=pltpu.CompilerParams(dimension_semantics=("parallel", "parallel")),
        name="merge",
    )(x, att, hg, hy, z, z, z, mod, mod, wb_bf, wo_bf)


def _mlp_kernel(x_ref, shl_ref, scl_ref, shc_ref, scc_ref, gl_ref, gc_ref, g_ref, w1_ref, w2_ref, o_ref,
                h_ref, acc_ref, *, n_lat, tm):
    t = pl.program_id(1)
    j = pl.program_id(2)

    @pl.when(j == 0)
    def _():
        h = _modulated_norm(x_ref[0], g_ref[...], shl_ref[0, 0], scl_ref[0, 0], shc_ref[0, 0], scc_ref[0, 0],
                            t * tm, n_lat)
        h_ref[...] = h.astype(BF16)
        acc_ref[...] = jnp.zeros_like(acc_ref)

    a = jnp.maximum(jnp.dot(h_ref[...], w1_ref[...], preferred_element_type=F32), 0.0)
    acc_ref[...] += jnp.dot((a * a).astype(BF16), w2_ref[...], preferred_element_type=F32)

    @pl.when(j == pl.num_programs(2) - 1)
    def _():
        row = t * tm + lax.broadcasted_iota(jnp.int32, (tm, 1), 0)
        gate = jnp.where(row >= n_lat, gc_ref[0, 0], gl_ref[0, 0])
        o_ref[0] = x_ref[0] + gate * acc_ref[...]


def _mlp(x, mod, layer, ctx_row, g, w1_bf, w2_bf, n_lat):
    bsz, seq, d = x.shape
    ff = w1_bf.shape[1]
    tm = _pick(seq, (768, 512, 256))
    tf = _pick(ff, (2048, 1024, 512, 128))
    return pl.pallas_call(
        functools.partial(_mlp_kernel, n_lat=n_lat, tm=tm),
        grid=(bsz, seq // tm, ff // tf),
        in_specs=[pl.BlockSpec((1, tm, d), lambda b, t, j: (b, t, 0))]
        + _mod_specs(layer, ctx_row, 3, 4, d, 3)
        + [pl.BlockSpec((1, 1, 1, d), lambda b, t, j: (layer, b, 0, 5)),
           pl.BlockSpec((1, 1, 1, d), lambda b, t, j: (layer, ctx_row, 0, 5)),
           pl.BlockSpec((1, d), lambda b, t, j: (0, 0)),
           pl.BlockSpec((d, tf), lambda b, t, j: (0, j)),
           pl.BlockSpec((tf, d), lambda b, t, j: (j, 0))],
        out_specs=pl.BlockSpec((1, tm, d), lambda b, t, j: (b, t, 0)),
        out_shape=jax.ShapeDtypeStruct((bsz, seq, d), F32),
        scratch_shapes=[pltpu.VMEM((tm, d), BF16), pltpu.VMEM((tm, d), F32)],
        compiler_params=pltpu.CompilerParams(dimension_semantics=("parallel", "parallel", "arbitrary")),
        name="mlp",
    )(x, mod, mod, mod, mod, mod, mod, g.reshape(1, d), w1_bf, w2_bf)


def kernel(x, c, ctx, c_ctx, ada_w, ada_b, norm1_g, norm2_g, w_in, q_norm_g, k_norm_g, hg_lb_raw, hg_norm_g,
           hy_short_w, hy_short_b, hy_filt_w1, hy_filt_b1, hy_filt_w2, hy_filt_b2, hy_filt_w3, hy_freq, hy_bias,
           w_branch, w_out, w_mlp1, w_mlp2):
    bsz, n_lat, d = x.shape
    n_ctx = ctx.shape[1]
    depth = ada_w.shape[0]
    gate_cols = 3 * d

    cvec = jnp.concatenate([c, c_ctx[None, :]], axis=0)
    ctx_row = bsz
    pad_rows = -cvec.shape[0] % 8
    cvec = jnp.pad(cvec, ((0, pad_rows), (0, 0)))
    mod = _adaln(cvec, ada_w, ada_b)
    mod = mod.reshape(depth, cvec.shape[0], 1, 6 * d)

    cos, sin_signed = _rope_tables(n_lat, n_ctx)
    filt_ctx, dft_ctx = _dft_matrices(n_ctx)
    filt_lat, mats, mats_t = _radix8_matrices(n_lat)

    xs = jnp.concatenate([x, ctx], axis=1)
    for l in range(depth):
        need_ctx = l < depth - 1
        w_l = w_in[l]
        w_bf = jnp.concatenate([w_l[:, -gate_cols:], w_l[:, :-gate_cols]], axis=1).astype(BF16)
        z = _inproj(xs, mod, l, ctx_row, norm1_g[l], w_bf, n_lat)

        att = _attention(z, cos, sin_signed, q_norm_g[l], k_norm_g[l], n_lat, need_ctx)
        hg = _hgrn(z, hg_lb_raw, hg_norm_g[l], l, n_lat, n_ctx, need_ctx)
        w1p = jnp.pad(hy_filt_w1[l], ((0, LANES - HY_EMB), (0, 0)))
        filt_args = (w1p, hy_filt_b1[l], hy_filt_w2[l], hy_filt_b2[l], hy_filt_w3[l], hy_freq[l])
        hy_ctx = None
        if need_ctx:
            hy_ctx = _hyena_dense(z, hy_short_w[l], hy_short_b[l], _hyena_filters(n_ctx, filt_ctx, *filt_args),
                                  dft_ctx, hy_bias[l], n_ctx, n_lat // n_ctx)
        hy = _hyena_radix8(z, hy_short_w[l], hy_short_b[l], _hyena_filters(n_lat, filt_lat, *filt_args),
                           mats, mats_t, hy_bias[l], n_lat, tail=hy_ctx)
        rows = n_lat + n_ctx if need_ctx else n_lat
        xs = _merge(xs, att, hg, hy, z, mod, l, ctx_row, w_branch[l].astype(BF16), w_out[l].astype(BF16), n_lat,
                    rows)
        xs = _mlp(xs, mod, l, ctx_row, norm2_g[l], w_mlp1[l].astype(BF16), w_mlp2[l].astype(BF16), n_lat)
    return xs
```
